```python
import math, functools
import jax, jax.numpy as jnp
from jax import lax
import numpy as np


D_MODEL = 1024
BATCH = 32
SEQ = 2048
DEPTH = 1
DEC_BATCH = 8
DEC_SEQ = 16
PAST_LEN = 1024

CHUNK = 64
Q_BLOCK = 128
MLSTM_WIDTH = D_MODEL // 2
DIFF_WIDTH = D_MODEL - MLSTM_WIDTH
MLSTM_V_DIM = 128
MLSTM_QK_DIM = MLSTM_V_DIM // 2
N_MLSTM_HEADS = MLSTM_WIDTH // MLSTM_V_DIM
DIFF_V_DIM = 128
DIFF_QK_DIM = DIFF_V_DIM // 2
N_DIFF_HEADS = DIFF_WIDTH // DIFF_V_DIM
ROPE_DIM = DIFF_QK_DIM // 4
ROPE_THETA = 500000.0
D_FF = 2816
N_ADA = 9
NORM_EPS = 1e-6
PROJ_SIZES = (N_MLSTM_HEADS * MLSTM_QK_DIM, N_MLSTM_HEADS * MLSTM_QK_DIM, N_MLSTM_HEADS * MLSTM_V_DIM,
              MLSTM_WIDTH, N_MLSTM_HEADS, N_MLSTM_HEADS,
              N_DIFF_HEADS * 2 * DIFF_QK_DIM, N_DIFF_HEADS * 2 * DIFF_QK_DIM, N_DIFF_HEADS * DIFF_V_DIM)
PROJ_WIDTH = (3 * N_MLSTM_HEADS * MLSTM_QK_DIM + N_MLSTM_HEADS * MLSTM_V_DIM + MLSTM_WIDTH + 2 * N_MLSTM_HEADS
              + 4 * N_DIFF_HEADS * DIFF_QK_DIM + N_DIFF_HEADS * DIFF_V_DIM) - N_MLSTM_HEADS * MLSTM_QK_DIM

kernel_name = 'hybrid_mlstm_diffattn_streaming_step'


def rms_norm(x, g):
    xf = x.astype(jnp.float32)
    y = xf * lax.rsqrt(jnp.mean(xf * xf, axis=-1, keepdims=True) + NORM_EPS)
    return (y * g.astype(jnp.float32)).astype(x.dtype)


def modulate(x, g, shift, scale):
    return rms_norm(x, g) * (1 + scale[:, None, :]) + shift[:, None, :]


def residual(x, out, g, gate, weight):
    return x + weight * gate[:, None, :] * rms_norm(out, g)


def swiglu(h, w_in, w_out):
    gate, up = jnp.split(h @ w_in, 2, axis=-1)
    return (jax.nn.silu(gate) * up) @ w_out


def ada_params(c, w_ada, b_ada):
    mod = jax.nn.silu(c) @ w_ada + b_ada
    return mod.reshape(c.shape[0], N_ADA, D_MODEL)


def partial_rope(x, pos):
    half = ROPE_DIM // 2
    inv_freq = ROPE_THETA ** (-jnp.arange(half, dtype=jnp.float32) * (2.0 / ROPE_DIM))
    ang = pos.astype(jnp.float32)[:, None] * inv_freq[None, :]
    cos = jnp.cos(ang)[None, :, None, None, :]
    sin = jnp.sin(ang)[None, :, None, None, :]
    xr = x[..., :ROPE_DIM].astype(jnp.float32)
    x1, x2 = xr[..., :half], xr[..., half:]
    rot = jnp.concatenate([x1 * cos - x2 * sin, x2 * cos + x1 * sin], axis=-1).astype(x.dtype)
    return jnp.concatenate([rot, x[..., ROPE_DIM:]], axis=-1)


def project(h, w_in, b_igate, b_fgate, pos):
    B, T, _ = h.shape
    f32 = jnp.float32
    idx = [int(i) for i in np.cumsum(PROJ_SIZES)[:-1]]
    mq, mk, mv, mo, mi, mf, dq, dk, dv = jnp.split(h @ w_in, idx, axis=-1)
    mq = mq.reshape(B, T, N_MLSTM_HEADS, MLSTM_QK_DIM).astype(f32)
    mk = mk.reshape(B, T, N_MLSTM_HEADS, MLSTM_QK_DIM).astype(f32) * (MLSTM_QK_DIM ** -0.5)
    mv = mv.reshape(B, T, N_MLSTM_HEADS, MLSTM_V_DIM).astype(f32)
    o_gate = jax.nn.sigmoid(mo)
    i_pre = (mi + b_igate).astype(f32)
    log_f = jax.nn.log_sigmoid((mf + b_fgate).astype(f32))
    dq = partial_rope(dq.reshape(B, T, N_DIFF_HEADS, 2, DIFF_QK_DIM), pos)
    dk = partial_rope(dk.reshape(B, T, N_DIFF_HEADS, 2, DIFF_QK_DIM), pos)
    dv = dv.reshape(B, T, N_DIFF_HEADS, DIFF_V_DIM)
    return (mq, mk, mv, i_pre, log_f), o_gate, (dq, dk, dv)


def mlstm_chunk(carry, inp):
    C, n, m = carry
    q, k, v, i_pre, log_f = inp
    L = q.shape[1]
    b = jnp.moveaxis(jnp.cumsum(log_f, axis=1), 1, 2)
    ig = jnp.moveaxis(i_pre, 1, 2)
    dmat = b[..., :, None] - b[..., None, :] + ig[..., None, :]
    causal = jnp.tril(jnp.ones((L, L), dtype=bool))
    dmat = jnp.where(causal, dmat, -jnp.inf)
    inter = b + m[..., None]
    m_t = jnp.maximum(inter, jnp.max(dmat, axis=-1))
    w_intra = jnp.exp(dmat - m_t[..., None])
    w_inter = jnp.exp(inter - m_t)
    s = jnp.einsum('blhd,bshd->bhls', q, k) * w_intra
    num = (jnp.einsum('bhls,bshv->blhv', s, v)
           + jnp.einsum('bhvd,blhd->blhv', C, q) * jnp.moveaxis(w_inter, 1, 2)[..., None])
    den = jnp.sum(s, axis=-1) + w_inter * jnp.einsum('bhd,blhd->bhl', n, q)
    den = jnp.maximum(jnp.abs(den), jnp.exp(-m_t))
    h = num / jnp.moveaxis(den, 1, 2)[..., None]
    m_last = m_t[..., -1]
    w_state = jnp.exp(b[..., -1] + m - m_last)
    w_rows = jnp.exp(b[..., -1:] - b + ig - m_last[..., None])
    C_new = w_state[..., None, None] * C + jnp.einsum('bhs,bshv,bshd->bhvd', w_rows, v, k)
    n_new = w_state[..., None] * n + jnp.einsum('bhs,bshd->bhd', w_rows, k)
    return (C_new, n_new, m_last), h


def mlstm_readout(h, o_gate, g):
    B, T = h.shape[:2]
    return rms_norm(h, g).reshape(B, T, MLSTM_WIDTH).astype(o_gate.dtype) * o_gate


def diff_lambda(lam_qk, layer):
    lam_init = 0.8 - 0.6 * math.exp(-0.3 * layer)
    lf = lam_qk.astype(jnp.float32)
    lam = jnp.exp(jnp.sum(lf[0] * lf[1])) - jnp.exp(jnp.sum(lf[2] * lf[3])) + lam_init
    return lam, lam_init


def diff_weights(scores, lam):
    p = jax.nn.softmax(scores, axis=-1)
    return p[:, :, 0] - lam * p[:, :, 1]


def diff_attn_prompt(q, k, v, lam):
    B, S = q.shape[:2]
    nb = S // Q_BLOCK
    q_blocks = jnp.moveaxis(q.reshape(B, nb, Q_BLOCK, N_DIFF_HEADS, 2, DIFF_QK_DIM), 1, 0)
    key_chunk = jnp.arange(S) // CHUNK
    vf = v.astype(jnp.float32)
    scale = DIFF_QK_DIM ** -0.5

    def one_block(args):
        qb, blk = args
        q_chunk = (blk * Q_BLOCK + jnp.arange(Q_BLOCK)) // CHUNK
        s = jnp.einsum('bqhcd,bkhcd->bhcqk', qb, k).astype(jnp.float32) * scale
        visible = key_chunk[None, :] <= q_chunk[:, None]
        s = jnp.where(visible, s, -jnp.inf)
        return jnp.einsum('bhqk,bkhd->bqhd', diff_weights(s, lam), vf)

    out = lax.map(one_block, (q_blocks, jnp.arange(nb)))
    return jnp.moveaxis(out, 0, 1).reshape(B, S, N_DIFF_HEADS, DIFF_V_DIM)


def diff_readout(o, g, lam_init):
    B, T = o.shape[:2]
    return (rms_norm(o, g) * (1 - lam_init)).reshape(B, T, DIFF_WIDTH)


def mixer_prompt(h, w_in, b_igate, b_fgate, g_mlstm, g_diff, lam, lam_init):
    B, S, _ = h.shape
    pos = jnp.arange(S)
    m_inputs, o_gate, (dq, dk, dv) = project(h, w_in, b_igate, b_fgate, pos)
    n_chunks = S // CHUNK
    chunked = tuple(jnp.moveaxis(a.reshape(B, n_chunks, CHUNK, *a.shape[2:]), 1, 0) for a in m_inputs)
    f32 = jnp.float32
    carry0 = (jnp.zeros((B, N_MLSTM_HEADS, MLSTM_V_DIM, MLSTM_QK_DIM), f32),
              jnp.zeros((B, N_MLSTM_HEADS, MLSTM_QK_DIM), f32),
              jnp.zeros((B, N_MLSTM_HEADS), f32))
    (C, n, m), hm = lax.scan(mlstm_chunk, carry0, chunked)
    hm = jnp.moveaxis(hm, 0, 1).reshape(B, S, N_MLSTM_HEADS, MLSTM_V_DIM)
    y_m = mlstm_readout(hm, o_gate, g_mlstm)
    y_d = diff_readout(diff_attn_prompt(dq, dk, dv, lam), g_diff, lam_init)
    heads = jnp.concatenate([y_m.astype(h.dtype), y_d.astype(h.dtype)], axis=-1)
    new_state = (dk.reshape(B, S, N_DIFF_HEADS, 2 * DIFF_QK_DIM), dv,
                 C.astype(h.dtype), n.astype(h.dtype), m.astype(h.dtype))
    return heads, new_state


def mixer_sample(h, cache_k, cache_v, state_C, state_n, state_m, w_in, b_igate, b_fgate,
                 g_mlstm, g_diff, lam, lam_init):
    B, T, _ = h.shape
    P = cache_k.shape[1]
    pos = P + jnp.arange(T)
    m_inputs, o_gate, (dq, dk, dv) = project(h, w_in, b_igate, b_fgate, pos)
    f32 = jnp.float32
    carry = (state_C.astype(f32), state_n.astype(f32), state_m.astype(f32))
    (C, n, m), hm = mlstm_chunk(carry, m_inputs)
    y_m = mlstm_readout(hm, o_gate, g_mlstm)
    k_all = jnp.concatenate([cache_k.reshape(B, P, N_DIFF_HEADS, 2, DIFF_QK_DIM), dk], axis=1)
    v_all = jnp.concatenate([cache_v, dv], axis=1)
    s = jnp.einsum('bqhcd,bkhcd->bhcqk', dq, k_all).astype(f32) * (DIFF_QK_DIM ** -0.5)
    o = jnp.einsum('bhqk,bkhd->bqhd', diff_weights(s, lam), v_all.astype(f32))
    y_d = diff_readout(o, g_diff, lam_init)
    heads = jnp.concatenate([y_m.astype(h.dtype), y_d.astype(h.dtype)], axis=-1)
    new_state = (dk.reshape(B, T, N_DIFF_HEADS, 2 * DIFF_QK_DIM), dv,
                 C.astype(h.dtype), n.astype(h.dtype), m.astype(h.dtype))
    return heads, new_state


def trunk_layer(x, c, mixer, w_ada, b_ada, g_norm, w_ffn1_in, w_ffn1_out, w_out, w_ffn2_in, w_ffn2_out):
    mod = ada_params(c, w_ada, b_ada)
    x = residual(x, swiglu(modulate(x, g_norm[0], mod[:, 0], mod[:, 1]), w_ffn1_in, w_ffn1_out),
                 g_norm[1], mod[:, 2], 0.5)
    heads, new_state = mixer(modulate(x, g_norm[2], mod[:, 3], mod[:, 4]))
    x = residual(x, heads @ w_out, g_norm[3], mod[:, 5], 1.0)
    x = residual(x, swiglu(modulate(x, g_norm[4], mod[:, 6], mod[:, 7]), w_ffn2_in, w_ffn2_out),
                 g_norm[5], mod[:, 8], 0.5)
    return x, new_state


def setup_inputs(seed: int = 0) -> dict:
    key = jax.random.key(seed)
    ks = jax.random.split(key, 24)
    f32 = jnp.float32

    def nrm(k, shape, s):
        return jax.random.normal(k, shape, f32) * s

    return {
        'x_prompt': nrm(ks[0], (BATCH, SEQ, D_MODEL), 1.0),
        'x_sample': nrm(ks[1], (DEC_BATCH, DEC_SEQ, D_MODEL), 1.0),
        'cache_k': nrm(ks[2], (DEPTH, DEC_BATCH, PAST_LEN, N_DIFF_HEADS, 2 * DIFF_QK_DIM), 1.0),
        'cache_v': nrm(ks[3], (DEPTH, DEC_BATCH, PAST_LEN, N_DIFF_HEADS, DIFF_V_DIM), 1.0),
        'state_C': nrm(ks[4], (DEPTH, DEC_BATCH, N_MLSTM_HEADS, MLSTM_V_DIM, MLSTM_QK_DIM), 0.3),
        'state_n': nrm(ks[5], (DEPTH, DEC_BATCH, N_MLSTM_HEADS, MLSTM_QK_DIM), 1.0),
        'state_m': nrm(ks[6], (DEPTH, DEC_BATCH, N_MLSTM_HEADS), 0.5),
        'c_prompt': nrm(ks[7], (BATCH, D_MODEL), 1.0),
        'c_sample': nrm(ks[8], (DEC_BATCH, D_MODEL), 1.0),
        'w_ada': nrm(ks[9], (DEPTH, D_MODEL, N_ADA * D_MODEL), D_MODEL ** -0.5),
        'b_ada': nrm(ks[10], (DEPTH, N_ADA * D_MODEL), 0.01),
        'g_norm': 1.0 + nrm(ks[11], (DEPTH, 6, D_MODEL), 0.05),
        'w_ffn1_in': nrm(ks[12], (DEPTH, D_MODEL, 2 * D_FF), D_MODEL ** -0.5),
        'w_ffn1_out': nrm(ks[13], (DEPTH, D_FF, D_MODEL), D_FF ** -0.5),
        'w_in': nrm(ks[14], (DEPTH, D_MODEL, PROJ_WIDTH), D_MODEL ** -0.5),
        'b_igate': nrm(ks[15], (DEPTH, N_MLSTM_HEADS), 0.1),
        'b_fgate': jnp.linspace(3.0, 6.0, N_MLSTM_HEADS, dtype=f32)[None, :] + nrm(ks[16], (DEPTH, N_MLSTM_HEADS), 0.1),
        'g_mlstm': 1.0 + nrm(ks[17], (DEPTH, N_MLSTM_HEADS, MLSTM_V_DIM), 0.05),
        'g_diff': 1.0 + nrm(ks[18], (DEPTH, N_DIFF_HEADS, DIFF_V_DIM), 0.05),
        'lambda_qk': nrm(ks[19], (DEPTH, 4, DIFF_QK_DIM), 0.1),
        'w_out': nrm(ks[20], (DEPTH, D_MODEL, D_MODEL), D_MODEL ** -0.5),
        'w_ffn2_in': nrm(ks[21], (DEPTH, D_MODEL, 2 * D_FF), D_MODEL ** -0.5),
        'w_ffn2_out': nrm(ks[22], (DEPTH, D_FF, D_MODEL), D_FF ** -0.5),
    }


def reference(x_prompt, x_sample, cache_k, cache_v, state_C, state_n, state_m, c_prompt, c_sample,
              w_ada, b_ada, g_norm, w_ffn1_in, w_ffn1_out, w_in, b_igate, b_fgate, g_mlstm, g_diff,
              lambda_qk, w_out, w_ffn2_in, w_ffn2_out):
    y_p, y_s = x_prompt, x_sample
    states_p, states_s = [], []
    for l in range(DEPTH):
        lam, lam_init = diff_lambda(lambda_qk[l], l)
        mix_p = functools.partial(mixer_prompt, w_in=w_in[l], b_igate=b_igate[l], b_fgate=b_fgate[l],
                                  g_mlstm=g_mlstm[l], g_diff=g_diff[l], lam=lam, lam_init=lam_init)
        mix_s = functools.partial(mixer_sample, cache_k=cache_k[l], cache_v=cache_v[l], state_C=state_C[l],
                                  state_n=state_n[l], state_m=state_m[l], w_in=w_in[l], b_igate=b_igate[l],
                                  b_fgate=b_fgate[l], g_mlstm=g_mlstm[l], g_diff=g_diff[l],
                                  lam=lam, lam_init=lam_init)
        y_p, st_p = trunk_layer(y_p, c_prompt, mix_p, w_ada[l], b_ada[l], g_norm[l], w_ffn1_in[l],
                                w_ffn1_out[l], w_out[l], w_ffn2_in[l], w_ffn2_out[l])
        y_s, st_s = trunk_layer(y_s, c_sample, mix_s, w_ada[l], b_ada[l], g_norm[l], w_ffn1_in[l],
                                w_ffn1_out[l], w_out[l], w_ffn2_in[l], w_ffn2_out[l])
        states_p.append(st_p)
        states_s.append(st_s)
    k_p, v_p, C_p, n_p, m_p = [jnp.stack(a, axis=0) for a in zip(*states_p)]
    k_s, v_s, C_s, n_s, m_s = [jnp.stack(a, axis=0) for a in zip(*states_s)]
    return (y_p, y_s, k_p, v_p, C_p, n_p, m_p, k_s, v_s, C_s, n_s, m_s)
```

```python
import functools

import jax
import jax.numpy as jnp
import numpy as np
from jax import lax
from jax.experimental import pallas as pl
from jax.experimental.pallas import tpu as pltpu

F32 = jnp.float32
BF16 = jnp.bfloat16

LANES = 128
MXU_N = 256
VMEM_LIMIT_BYTES = 56 << 20
NORM_EPS = 1e-6
ROPE_THETA = 500000.0
N_ADA = 9
CHUNK = 64

_NT = (((1,), (1,)), ((), ()))
_TN = (((0,), (0,)), ((), ()))


def _dot(a, b):
    return jnp.dot(a, b, preferred_element_type=F32)


def _dot_nt(a, b):
    return lax.dot_general(a, b, _NT, preferred_element_type=F32)


def _dot_tn(a, b):
    return lax.dot_general(a, b, _TN, preferred_element_type=F32)


def _rms(x):
    return x * lax.rsqrt(jnp.mean(x * x, axis=-1, keepdims=True) + NORM_EPS)


def _silu(x):
    return x * jax.nn.sigmoid(x)


def _params(*sem):
    return pltpu.CompilerParams(dimension_semantics=sem, vmem_limit_bytes=VMEM_LIMIT_BYTES)


def _resident(shape):
    nd = len(shape)
    return pl.BlockSpec(shape, lambda *_: (0,) * nd, pipeline_mode=pl.Buffered(1))


def _ada_kernel(c_ref, w_ref, b_ref, o_ref):
    a = _silu(c_ref[...]).astype(BF16)
    o_ref[...] = _dot(a, w_ref[...].astype(BF16)) + b_ref[...]


def _ada(c, w_ada, b_ada):
    n, d = c.shape
    width = w_ada.shape[1]
    tn = width // 8
    return pl.pallas_call(
        _ada_kernel,
        grid=(width // tn,),
        in_specs=[pl.BlockSpec((n, d), lambda j: (0, 0)),
                  pl.BlockSpec((d, tn), lambda j: (0, j)),
                  pl.BlockSpec((1, tn), lambda j: (0, j))],
        out_specs=pl.BlockSpec((n, tn), lambda j: (0, j)),
        out_shape=jax.ShapeDtypeStruct((n, width), F32),
        compiler_params=_params("parallel"),
        name="ada",
    )(c, w_ada, b_ada.reshape(1, width))


def _ffn_kernel(*refs, nb, t, pre_proj, g_idx, m_idx):
    if pre_proj:
        x_ref, ym_ref, yd_ref, mod_ref, g_ref, wo_ref, w_in_ref, w_out_ref, o_ref, h_ref, act_ref = refs
    else:
        x_ref, mod_ref, g_ref, w_in_ref, w_out_ref, o_ref, h_ref, act_ref = refs
    d = x_ref.shape[-1]
    rows = nb * t
    x = x_ref[...]

    def mrow(i):
        return mod_ref[:, i:i + 1, :]

    def grow(i):
        return g_ref[i:i + 1, :][None]

    if pre_proj:
        heads = jnp.concatenate([ym_ref[...], yd_ref[...]], axis=-1).reshape(rows, d)
        o = _dot(heads, wo_ref[...]).reshape(nb, t, d)
        x = x + mrow(5) * (_rms(o) * grow(3))

    g_a, g_b = g_idx
    i_shift, i_scale, i_gate = m_idx
    h = _rms(x) * grow(g_a) * (1.0 + mrow(i_scale)) + mrow(i_shift)
    h_ref[...] = h.astype(BF16).reshape(rows, d)

    n_groups = w_out_ref.shape[0] // LANES
    for j in range(n_groups):
        gu = _dot(h_ref[...], w_in_ref[:, j * MXU_N:(j + 1) * MXU_N])
        act = _silu(gu[:, :LANES]) * gu[:, LANES:]
        act_ref[:, j * LANES:(j + 1) * LANES] = act.astype(BF16)
    out = _dot(act_ref[...], w_out_ref[...]).reshape(nb, t, d)
    o_ref[...] = x + (0.5 * mrow(i_gate)) * (_rms(out) * grow(g_b))


def _ffn_block(x, mod, g_norm, w_in_r, w_out, *, nb, t, g_idx, m_idx, heads=None, w_o=None):
    b, s, d = x.shape
    f = w_out.shape[0]
    pre_proj = heads is not None
    grid = (b // nb, s // t)
    tok = lambda width: pl.BlockSpec((nb, t, width), lambda i, j: (i, j, 0))
    in_specs = [tok(d)]
    args = [x]
    if pre_proj:
        ym, yd = heads
        in_specs += [tok(ym.shape[-1]), tok(yd.shape[-1])]
        args += [ym, yd]
    in_specs += [pl.BlockSpec((nb, N_ADA, d), lambda i, j: (i, 0, 0)), _resident(g_norm.shape)]
    args += [mod, g_norm]
    if pre_proj:
        in_specs.append(_resident(w_o.shape))
        args.append(w_o)
    in_specs += [_resident(w_in_r.shape), _resident(w_out.shape)]
    args += [w_in_r, w_out]
    return pl.pallas_call(
        functools.partial(_ffn_kernel, nb=nb, t=t, pre_proj=pre_proj, g_idx=g_idx, m_idx=m_idx),
        grid=grid,
        in_specs=in_specs,
        out_specs=tok(d),
        out_shape=jax.ShapeDtypeStruct((b, s, d), F32),
        scratch_shapes=[pltpu.VMEM((nb * t, d), BF16), pltpu.VMEM((nb * t, f), BF16)],
        compiler_params=_params("parallel", "parallel"),
        name="ffn_post" if pre_proj else "ffn_pre",
    )(*args)


def _proj_kernel(x_ref, mod_ref, g_ref, w_ref, bias_ref, cos_ref, sa_ref, sb_ref,
                 mq_ref, mk_ref, mv_ref, og_ref, dq_ref, dk_ref, dv_ref, gt_ref, h_ref,
                 *, nb, t, hm, widths, k_scale, q_scale):
    d = x_ref.shape[-1]
    rows = nb * t
    h = _rms(x_ref[...]) * g_ref[2:3, :][None] * (1.0 + mod_ref[:, 4:5, :]) + mod_ref[:, 3:4, :]
    h_ref[...] = h.astype(BF16).reshape(rows, d)

    def mm(c0, width):
        return _dot(h_ref[...], w_ref[:, c0:c0 + width])

    def put(ref, c, val):
        width = val.shape[-1]
        ref[:, :, c:c + width] = val.astype(ref.dtype).reshape(nb, t, width)

    def rope(xb):
        x3 = xb.reshape(nb, t, LANES)
        up = pltpu.roll(xb, LANES - 8, 1).reshape(nb, t, LANES)
        dn = pltpu.roll(xb, 8, 1).reshape(nb, t, LANES)
        r = x3 * cos_ref[...][None] + up * sa_ref[...][None] + dn * sb_ref[...][None]
        return r.reshape(rows, LANES)

    w_mq, w_mk, w_mv, w_mo, w_dq, w_dk, w_dv = widths
    c0 = 0
    for c in range(0, w_mq, MXU_N):
        put(mq_ref, c, mm(c0 + c, min(MXU_N, w_mq - c)))
    c0 += w_mq
    for c in range(0, w_mk, MXU_N):
        put(mk_ref, c, mm(c0 + c, min(MXU_N, w_mk - c)) * k_scale)
    c0 += w_mk
    for c in range(0, w_mv, MXU_N):
        put(mv_ref, c, mm(c0 + c, min(MXU_N, w_mv - c)))
    c0 += w_mv
    for c in range(0, w_mo, MXU_N):
        put(og_ref, c, jax.nn.sigmoid(mm(c0 + c, min(MXU_N, w_mo - c))))
    c0 += w_mo
    for c in range(0, w_dq, LANES):
        put(dq_ref, c, rope(mm(c0 + c, LANES)) * q_scale)
    c0 += w_dq
    for c in range(0, w_dk, LANES):
        put(dk_ref, c, rope(mm(c0 + c, LANES)))
    c0 += w_dk
    for c in range(0, w_dv, MXU_N):
        put(dv_ref, c, mm(c0 + c, min(MXU_N, w_dv - c)))
    c0 += w_dv
    z = mm(c0, LANES) + bias_ref[...]
    log_sig = jnp.minimum(z, 0.0) - jnp.log1p(jnp.exp(-jnp.abs(z)))
    lane = lax.broadcasted_iota(jnp.int32, z.shape, 1)
    put(gt_ref, 0, jnp.where(lane < hm, z, jnp.where(lane < 2 * hm, log_sig, 0.0)))


def _project(x, mod, g_norm, w_proj, bias, tables, *, nb, t, hm, widths, k_scale, q_scale):
    b, s, d = x.shape
    cos_t, sa_t, sb_t = tables
    grid = (b // nb, s // t)
    tok = lambda width: pl.BlockSpec((nb, t, width), lambda i, j: (i, j, 0))
    tab = pl.BlockSpec((t, LANES), lambda i, j: (j, 0))
    w_mq, w_mk, w_mv, w_mo, w_dq, w_dk, w_dv = widths
    out_widths = [(w_mq, BF16), (w_mk, BF16), (w_mv, BF16), (w_mo, BF16),
                  (w_dq, BF16), (w_dk, F32), (w_dv, F32), (LANES, F32)]
    return pl.pallas_call(
        functools.partial(_proj_kernel, nb=nb, t=t, hm=hm, widths=widths,
                          k_scale=k_scale, q_scale=q_scale),
        grid=grid,
        in_specs=[tok(d), pl.BlockSpec((nb, N_ADA, d), lambda i, j: (i, 0, 0)),
                  _resident(g_norm.shape), _resident(w_proj.shape), _resident(bias.shape),
                  tab, tab, tab],
        out_specs=[tok(w) for w, _ in out_widths],
        out_shape=[jax.ShapeDtypeStruct((b, s, w), dt) for w, dt in out_widths],
        scratch_shapes=[pltpu.VMEM((nb * t, d), BF16)],
        compiler_params=_params("parallel", "parallel"),
        name="in_proj",
    )(x, mod, g_norm, w_proj, bias, cos_t, sa_t, sb_t)


def _mlstm_kernel(*refs, L, hm, dk, dv, has_init):
    if has_init:
        (q_ref, k_ref, v_ref, og_ref, gt_ref, gm_ref, c0_ref, n0_ref, m0_ref,
         y_ref, c_ref, n_ref, m_ref) = refs
    else:
        q_ref, k_ref, v_ref, og_ref, gt_ref, gm_ref, y_ref, c_ref, n_ref, m_ref = refs

    @pl.when(pl.program_id(1) == 0)
    def _():
        if has_init:
            c_ref[...] = c0_ref[...]
            n_ref[...] = n0_ref[...]
            m_ref[...] = m0_ref[...]
        else:
            c_ref[...] = jnp.zeros_like(c_ref)
            n_ref[...] = jnp.zeros_like(n_ref)
            m_ref[...] = jnp.zeros_like(m_ref)

    gates = gt_ref[0]
    row = lax.broadcasted_iota(jnp.int32, (L, L), 0)
    col = lax.broadcasted_iota(jnp.int32, (L, L), 1)
    causal = col <= row
    diag = col == row

    def as_row(col_vec):
        return jnp.sum(jnp.where(diag, col_vec, 0.0), axis=0, keepdims=True)

    for h in range(hm):
        ig_col = gates[:, h:h + 1]
        lf_row = as_row(gates[:, hm + h:hm + h + 1])
        b_col = jnp.sum(jnp.where(causal, lf_row, 0.0), axis=-1, keepdims=True)
        r_row = as_row(ig_col - b_col)
        m_prev = m_ref[0, h:h + 1, :]
        c_h = c_ref[0, h]
        n_h = n_ref[0, h:h + 1, :]
        q_h = q_ref[0, :, h * dk:(h + 1) * dk]
        k_h = k_ref[0, :, h * dk:(h + 1) * dk]
        v_h = v_ref[0, :, h * dv:(h + 1) * dv]

        dmat = jnp.where(causal, b_col + r_row, -jnp.inf)
        inter = b_col + m_prev
        m_t = jnp.maximum(inter, jnp.max(dmat, axis=-1, keepdims=True))
        w_inter = jnp.exp(inter - m_t)
        s = _dot_nt(q_h, k_h) * jnp.exp(dmat - m_t)
        num = _dot(s.astype(BF16), v_h) + _dot_nt(q_h, c_h.astype(BF16)) * w_inter
        qn = jnp.sum(q_h.astype(F32) * n_h, axis=-1, keepdims=True)
        den = jnp.sum(s, axis=-1, keepdims=True) + w_inter * qn
        den = jnp.maximum(jnp.abs(den), jnp.exp(-m_t))
        hid = num / den
        y = _rms(hid) * gm_ref[h:h + 1, :] * og_ref[0, :, h * dv:(h + 1) * dv].astype(F32)
        y_ref[0, :, h * dv:(h + 1) * dv] = y.astype(y_ref.dtype)

        m_last = m_t[L - 1:L, :]
        b_last = b_col[L - 1:L, :]
        w_state = jnp.exp(b_last + m_prev - m_last)
        w_rows = jnp.exp(b_last - b_col + ig_col - m_last)
        kw = k_h.astype(F32) * w_rows
        c_ref[0, h] = w_state * c_h + _dot_tn(v_h, kw.astype(BF16))
        n_ref[0, h:h + 1, :] = w_state * n_h + jnp.sum(kw, axis=0, keepdims=True)
        m_ref[0, h:h + 1, :] = m_last


def _mlstm(mq, mk, mv, og, gates, g_mlstm, *, L, init=None):
    b, s, _ = mq.shape
    hm, dv = g_mlstm.shape
    dk = mq.shape[-1] // hm
    has_init = init is not None
    tok = lambda width: pl.BlockSpec((1, L, width), lambda i, j: (i, j, 0))
    st_c = pl.BlockSpec((1, hm, dv, dk), lambda i, j: (i, 0, 0, 0))
    st_n = pl.BlockSpec((1, hm, dk), lambda i, j: (i, 0, 0))
    st_m = pl.BlockSpec((1, hm, 1), lambda i, j: (i, 0, 0))
    in_specs = [tok(hm * dk), tok(hm * dk), tok(hm * dv), tok(hm * dv), tok(LANES),
                _resident(g_mlstm.shape)]
    args = [mq, mk, mv, og, gates, g_mlstm]
    if has_init:
        in_specs += [st_c, st_n, st_m]
        args += list(init)
    return pl.pallas_call(
        functools.partial(_mlstm_kernel, L=L, hm=hm, dk=dk, dv=dv, has_init=has_init),
        grid=(b, s // L),
        in_specs=in_specs,
        out_specs=[tok(hm * dv), st_c, st_n, st_m],
        out_shape=[jax.ShapeDtypeStruct((b, s, hm * dv), BF16),
                   jax.ShapeDtypeStruct((b, hm, dv, dk), F32),
                   jax.ShapeDtypeStruct((b, hm, dk), F32),
                   jax.ShapeDtypeStruct((b, hm, 1), F32)],
        compiler_params=_params("parallel", "arbitrary"),
        name="mlstm_init" if has_init else "mlstm",
    )(*args)


def _diff_lambda(lam_ref, layer):
    lq = lam_ref[...]
    a = jnp.sum(lq[0:1, :] * lq[1:2, :], axis=-1, keepdims=True)
    b = jnp.sum(lq[2:3, :] * lq[3:4, :], axis=-1, keepdims=True)
    lam_init = 0.8 - 0.6 * float(np.exp(-0.3 * layer))
    return jnp.exp(a) - jnp.exp(b) + lam_init, lam_init


def _split_maps(q):
    lane = lax.broadcasted_iota(jnp.int32, q.shape, 1)
    zero = jnp.zeros_like(q)
    half = q.shape[-1] // 2
    return jnp.where(lane < half, q, zero), jnp.where(lane >= half, q, zero)


def _attn_prompt_kernel(q_ref, k_ref, v_ref, lam_ref, g_ref, o_ref, kb_ref, vt_ref, *, tq, layer):
    qi = pl.program_id(2)
    s_len, dv = k_ref.shape[1], k_ref.shape[2]

    @pl.when(qi == 0)
    def _():
        for c in range(s_len // tq):
            kb_ref[c] = k_ref[0, c * tq:(c + 1) * tq, :].astype(BF16)
            vt_ref[c] = v_ref[0, c * tq:(c + 1) * tq, :].T.astype(BF16)

    lam, lam_init = _diff_lambda(lam_ref, layer)
    q_maps = _split_maps(q_ref[0])

    def block(j, carry, masked):
        k_blk = kb_ref[j]
        vt_blk = vt_ref[j]
        out = []
        for c in range(2):
            m, l, acc = carry[3 * c:3 * c + 3]
            s = _dot_nt(k_blk, q_maps[c])
            if masked:
                key_chunk = lax.broadcasted_iota(jnp.int32, s.shape, 0) // CHUNK
                qry_chunk = lax.broadcasted_iota(jnp.int32, s.shape, 1) // CHUNK
                s = jnp.where(key_chunk <= qry_chunk, s, -jnp.inf)
            m_new = jnp.maximum(m, jnp.max(s, axis=0, keepdims=True))
            alpha = jnp.exp(m - m_new)
            p = jnp.exp(s - m_new)
            l = alpha * l + jnp.sum(p, axis=0, keepdims=True)
            acc = alpha * acc + _dot(vt_blk, p.astype(BF16))
            out += [m_new, l, acc]
        return tuple(out)

    init = (jnp.full((1, tq), -jnp.inf, F32), jnp.zeros((1, tq), F32), jnp.zeros((dv, tq), F32)) * 2
    carry = lax.fori_loop(0, qi, lambda j, c: block(j, c, False), init)
    _, l1, acc1, _, l2, acc2 = block(qi, carry, True)
    o_t = acc1 / l1 - lam * (acc2 / l2)
    o = o_t.T
    o_ref[0] = (_rms(o) * g_ref[0] * (1.0 - lam_init)).astype(o_ref.dtype)


def _attn_prompt(dq, dk, dv, lambda_qk, g_diff, *, tq, layer):
    b, s, _ = dq.shape
    hd, dvh = g_diff.shape
    return pl.pallas_call(
        functools.partial(_attn_prompt_kernel, tq=tq, layer=layer),
        grid=(b, hd, s // tq),
        in_specs=[pl.BlockSpec((1, tq, dvh), lambda i, h, j: (i, j, h)),
                  pl.BlockSpec((1, s, dvh), lambda i, h, j: (i, 0, h)),
                  pl.BlockSpec((1, s, dvh), lambda i, h, j: (i, 0, h)),
                  _resident(lambda_qk.shape),
                  pl.BlockSpec((1, 1, dvh), lambda i, h, j: (h, 0, 0))],
        out_specs=pl.BlockSpec((1, tq, dvh), lambda i, h, j: (i, j, h)),
        out_shape=jax.ShapeDtypeStruct((b, s, hd * dvh), BF16),
        scratch_shapes=[pltpu.VMEM((s // tq, tq, dvh), BF16), pltpu.VMEM((s // tq, dvh, tq), BF16)],
        compiler_params=_params("parallel", "parallel", "arbitrary"),
        name="diff_attn_prompt",
    )(dq, dk, dv, lambda_qk, g_diff.reshape(hd, 1, dvh))


def _attn_sample_kernel(q_ref, kc_ref, vc_ref, kn_ref, vn_ref, lam_ref, g_ref, o_ref, *, layer):
    lam, lam_init = _diff_lambda(lam_ref, layer)
    q_maps = _split_maps(q_ref[0])
    k_c = kc_ref[0].astype(BF16)
    k_n = kn_ref[0].astype(BF16)
    v_c = vc_ref[0].astype(BF16)
    v_n = vn_ref[0].astype(BF16)
    outs = []
    for c in range(2):
        s_c = _dot_nt(q_maps[c], k_c)
        s_n = _dot_nt(q_maps[c], k_n)
        m = jnp.maximum(jnp.max(s_c, axis=-1, keepdims=True), jnp.max(s_n, axis=-1, keepdims=True))
        p_c = jnp.exp(s_c - m)
        p_n = jnp.exp(s_n - m)
        l = jnp.sum(p_c, axis=-1, keepdims=True) + jnp.sum(p_n, axis=-1, keepdims=True)
        outs.append((_dot(p_c.astype(BF16), v_c) + _dot(p_n.astype(BF16), v_n)) / l)
    o = outs[0] - lam * outs[1]
    o_ref[0] = (_rms(o) * g_ref[0] * (1.0 - lam_init)).astype(o_ref.dtype)


def _attn_sample(dq, dk, dv, cache_k, cache_v, lambda_qk, g_diff, *, layer):
    b, t, _ = dq.shape
    p = cache_k.shape[1]
    hd, dvh = g_diff.shape
    new = pl.BlockSpec((1, t, dvh), lambda i, h: (i, 0, h))
    old = pl.BlockSpec((1, p, dvh), lambda i, h: (i, 0, h))
    return pl.pallas_call(
        functools.partial(_attn_sample_kernel, layer=layer),
        grid=(b, hd),
        in_specs=[new, old, old, new, new, _resident(lambda_qk.shape),
                  pl.BlockSpec((1, 1, dvh), lambda i, h: (h, 0, 0))],
        out_specs=new,
        out_shape=jax.ShapeDtypeStruct((b, t, hd * dvh), BF16),
        compiler_params=_params("parallel", "parallel"),
        name="diff_attn_sample",
    )(dq, cache_k, cache_v, dk, dv, lambda_qk, g_diff.reshape(hd, 1, dvh))


def _interleave_gate_up(w_in):
    d, two_f = w_in.shape
    n = two_f // 2 // LANES
    return w_in.reshape(d, 2, n, LANES).transpose(0, 2, 1, 3).reshape(d, two_f).astype(BF16)


def _rope_tables(pos, dqk):
    rope_dim = dqk // 4
    half = rope_dim // 2
    inv_freq = ROPE_THETA ** (-jnp.arange(half, dtype=F32) * (2.0 / rope_dim))
    ang = pos.astype(F32)[:, None] * inv_freq[None, :]
    cos, sin = jnp.cos(ang), jnp.sin(ang)
    n = pos.shape[0]
    pad = jnp.zeros((n, dqk - rope_dim), F32)
    zero = jnp.zeros((n, half), F32)
    cos_t = jnp.concatenate([cos, cos, pad + 1.0], axis=1)
    sa_t = jnp.concatenate([-sin, zero, pad], axis=1)
    sb_t = jnp.concatenate([zero, sin, pad], axis=1)
    rep = LANES // dqk
    return tuple(jnp.tile(a, (1, rep)) for a in (cos_t, sa_t, sb_t))


def _layer(x, c_mod, pos, attend, mlstm_init, weights, dims, *, nb, t, L, layer):
    (g_norm, w1_in, w1_out, w_proj, bias, g_mlstm, g_diff, lambda_qk, w_o, w2_in, w2_out) = weights
    hm, dk_m, widths = dims
    x1 = _ffn_block(x, c_mod, g_norm, w1_in, w1_out, nb=nb, t=t, g_idx=(0, 1), m_idx=(0, 1, 2))
    dqk = g_diff.shape[1] // 2
    mq, mk, mv, og, dq, dk, dv, gates = _project(
        x1, c_mod, g_norm, w_proj, bias, _rope_tables(pos, dqk), nb=nb, t=t, hm=hm, widths=widths,
        k_scale=dk_m ** -0.5, q_scale=dqk ** -0.5)
    y_m, c_new, n_new, m_new = _mlstm(mq, mk, mv, og, gates, g_mlstm, L=L, init=mlstm_init)
    y_d = attend(dq, dk, dv)
    y = _ffn_block(x1, c_mod, g_norm, w2_in, w2_out, nb=nb, t=t, g_idx=(4, 5), m_idx=(6, 7, 8),
                   heads=(y_m, y_d), w_o=w_o)
    return y, (dk, dv, c_new, n_new, m_new)


def kernel(x_prompt, x_sample, cache_k, cache_v, state_C, state_n, state_m, c_prompt, c_sample,
           w_ada, b_ada, g_norm, w_ffn1_in, w_ffn1_out, w_in, b_igate, b_fgate, g_mlstm, g_diff,
           lambda_qk, w_out, w_ffn2_in, w_ffn2_out):
    depth = w_ada.shape[0]
    bp, sp, d = x_prompt.shape
    bs, ts, _ = x_sample.shape
    past = cache_k.shape[2]
    hm, mv_dim = g_mlstm.shape[1:]
    hd, dv_dim = g_diff.shape[1:]
    mqk = mv_dim // 2
    dqk = dv_dim // 2
    sizes = [hm * mqk, hm * mqk, hm * mv_dim, hm * mv_dim, hm, hm, hd * 2 * dqk, hd * 2 * dqk, hd * dv_dim]
    offs = np.concatenate([[0], np.cumsum(sizes)])
    widths = tuple(sizes[i] for i in (0, 1, 2, 3, 6, 7, 8))
    t_prompt = min(512, sp)
    l_prompt = min(256, sp)

    y_p, y_s = x_prompt, x_sample
    states_p, states_s = [], []
    for l in range(depth):
        mod = _ada(jnp.concatenate([c_prompt, c_sample], axis=0), w_ada[l], b_ada[l])
        mod = mod.reshape(bp + bs, N_ADA, d)
        cols = [w_in[l][:, offs[i]:offs[i + 1]] for i in range(9)]
        gate_cols = jnp.concatenate([cols[4], cols[5], jnp.zeros((d, LANES - 2 * hm), F32)], axis=1)
        w_proj = jnp.concatenate([cols[i] for i in (0, 1, 2, 3, 6, 7, 8)] + [gate_cols], axis=1).astype(BF16)
        bias = jnp.concatenate([b_igate[l], b_fgate[l], jnp.zeros((LANES - 2 * hm,), F32)]).reshape(1, LANES)
        weights = (g_norm[l], _interleave_gate_up(w_ffn1_in[l]), w_ffn1_out[l].astype(BF16), w_proj, bias,
                   g_mlstm[l], g_diff[l], lambda_qk[l], w_out[l].astype(BF16),
                   _interleave_gate_up(w_ffn2_in[l]), w_ffn2_out[l].astype(BF16))
        dims = (hm, mqk, widths)

        attend_p = functools.partial(_attn_prompt, lambda_qk=lambda_qk[l], g_diff=g_diff[l],
                                     tq=min(256, sp), layer=l)
        y_p, st_p = _layer(y_p, mod[:bp], jnp.arange(sp), attend_p, None, weights, dims,
                           nb=1, t=t_prompt, L=l_prompt, layer=l)

        ck = cache_k[l].reshape(bs, past, hd * 2 * dqk)
        cv = cache_v[l].reshape(bs, past, hd * dv_dim)
        attend_s = functools.partial(_attn_sample, cache_k=ck, cache_v=cv, lambda_qk=lambda_qk[l],
                                     g_diff=g_diff[l], layer=l)
        init_s = (state_C[l], state_n[l], state_m[l].reshape(bs, hm, 1))
        y_s, st_s = _layer(y_s, mod[bp:], past + jnp.arange(ts), attend_s, init_s, weights, dims,
                           nb=bs, t=ts, L=ts, layer=l)
        states_p.append(st_p)
        states_s.append(st_s)

    def stack(states, b, s):
        k, v, c, n, m = [jnp.stack(a, axis=0) for a in zip(*states)]
        return (k.reshape(depth, b, s, hd, 2 * dqk), v.reshape(depth, b, s, hd, dv_dim),
                c, n, m.reshape(depth, b, hm))

    return (y_p, y_s) + stack(states_p, bp, sp) + stack(states_s, bs, ts)
```

```python
import functools

import jax
import jax.numpy as jnp
import numpy as np
from jax import lax
from jax.experimental import pallas as pl
from jax.experimental.pallas import tpu as pltpu

F32 = jnp.float32
BF16 = jnp.bfloat16

LANES = 128
BF16_ROWS = 16
MXU_N = 256
VMEM_LIMIT_BYTES = 56 << 20
NORM_EPS = 1e-6
ROPE_THETA = 500000.0
N_ADA = 9
CHUNK = 64

_NT = (((1,), (1,)), ((), ()))
_TN = (((0,), (0,)), ((), ()))


def _dot(a, b):
    return jnp.dot(a, b, preferred_element_type=F32)


def _dot_nt(a, b):
    return lax.dot_general(a, b, _NT, preferred_element_type=F32)


def _dot_tn(a, b):
    return lax.dot_general(a, b, _TN, preferred_element_type=F32)


def _rms(x):
    return x * lax.rsqrt(jnp.mean(x * x, axis=-1, keepdims=True) + NORM_EPS)


def _silu(x):
    return x * jax.nn.sigmoid(x)


def _params(*sem):
    return pltpu.CompilerParams(dimension_semantics=sem, vmem_limit_bytes=VMEM_LIMIT_BYTES)


def _resident(shape):
    nd = len(shape)
    return pl.BlockSpec(shape, lambda *_: (0,) * nd, pipeline_mode=pl.Buffered(1))


def _ada_kernel(c_ref, w_ref, b_ref, o_ref):
    a = _silu(c_ref[...]).astype(BF16)
    o_ref[...] = _dot(a, w_ref[...].astype(BF16)) + b_ref[...]


def _ada(c, w_ada, b_ada):
    n, d = c.shape
    width = w_ada.shape[1]
    tn = width // 8
    return pl.pallas_call(
        _ada_kernel,
        grid=(width // tn,),
        in_specs=[pl.BlockSpec((n, d), lambda j: (0, 0)),
                  pl.BlockSpec((d, tn), lambda j: (0, j)),
                  pl.BlockSpec((1, tn), lambda j: (0, j))],
        out_specs=pl.BlockSpec((n, tn), lambda j: (0, j)),
        out_shape=jax.ShapeDtypeStruct((n, width), F32),
        compiler_params=_params("parallel"),
        name="ada",
    )(c, w_ada, b_ada.reshape(1, width))


def _ffn_kernel(*refs, nb, t, pre_proj, g_idx, m_idx):
    if pre_proj:
        x_ref, ym_ref, yd_ref, mod_ref, g_ref, wo_ref, w_in_ref, w_out_ref, o_ref, h_ref, act_ref = refs
    else:
        x_ref, mod_ref, g_ref, w_in_ref, w_out_ref, o_ref, h_ref, act_ref = refs
    d = x_ref.shape[-1]
    rows = nb * t
    x = x_ref[...]

    def mrow(i):
        return mod_ref[:, i:i + 1, :]

    def grow(i):
        return g_ref[i:i + 1, :][None]

    if pre_proj:
        heads = jnp.concatenate([ym_ref[...], yd_ref[...]], axis=-1).reshape(rows, d)
        o = _dot(heads, wo_ref[...]).reshape(nb, t, d)
        x = x + mrow(5) * (_rms(o) * grow(3))

    g_a, g_b = g_idx
    i_shift, i_scale, i_gate = m_idx
    h = _rms(x) * grow(g_a) * (1.0 + mrow(i_scale)) + mrow(i_shift)
    h_ref[...] = h.astype(BF16).reshape(rows, d)

    n_groups = w_out_ref.shape[0] // LANES
    for j in range(n_groups):
        gu = _dot(h_ref[...], w_in_ref[:, j * MXU_N:(j + 1) * MXU_N])
        act = _silu(gu[:, :LANES]) * gu[:, LANES:]
        act_ref[:, j * LANES:(j + 1) * LANES] = act.astype(BF16)
    out = _dot(act_ref[...], w_out_ref[...]).reshape(nb, t, d)
    o_ref[...] = x + (0.5 * mrow(i_gate)) * (_rms(out) * grow(g_b))


def _ffn_block(x, mod, g_norm, w_in_r, w_out, *, nb, t, g_idx, m_idx, heads=None, w_o=None):
    b, s, d = x.shape
    f = w_out.shape[0]
    pre_proj = heads is not None
    grid = (b // nb, s // t)
    tok = lambda width: pl.BlockSpec((nb, t, width), lambda i, j: (i, j, 0))
    in_specs = [tok(d)]
    args = [x]
    if pre_proj:
        ym, yd = heads
        in_specs += [tok(ym.shape[-1]), tok(yd.shape[-1])]
        args += [ym, yd]
    in_specs += [pl.BlockSpec((nb, N_ADA, d), lambda i, j: (i, 0, 0)), _resident(g_norm.shape)]
    args += [mod, g_norm]
    if pre_proj:
        in_specs.append(_resident(w_o.shape))
        args.append(w_o)
    in_specs += [_resident(w_in_r.shape), _resident(w_out.shape)]
    args += [w_in_r, w_out]
    return pl.pallas_call(
        functools.partial(_ffn_kernel, nb=nb, t=t, pre_proj=pre_proj, g_idx=g_idx, m_idx=m_idx),
        grid=grid,
        in_specs=in_specs,
        out_specs=tok(d),
        out_shape=jax.ShapeDtypeStruct((b, s, d), F32),
        scratch_shapes=[pltpu.VMEM((nb * t, d), BF16), pltpu.VMEM((nb * t, f), BF16)],
        compiler_params=_params("parallel", "parallel"),
        name="ffn_post" if pre_proj else "ffn_pre",
    )(*args)


def _proj_kernel(x_ref, mod_ref, g_ref, w_ref, wv_ref, bias_ref, cos_ref, sa_ref, sb_ref,
                 mq_ref, mk_ref, mv_ref, og_ref, dq_ref, dk_ref, dv_ref, gt_ref, h_ref,
                 *, nb, t, hm, widths, k_scale, q_scale, v_transposed):
    d = x_ref.shape[-1]
    rows = nb * t
    h = _rms(x_ref[...]) * g_ref[2:3, :][None] * (1.0 + mod_ref[:, 4:5, :]) + mod_ref[:, 3:4, :]
    h_ref[...] = h.astype(BF16).reshape(rows, d)

    def mm(c0, width):
        return _dot(h_ref[...], w_ref[:, c0:c0 + width])

    def put(ref, c, val):
        width = val.shape[-1]
        ref[:, :, c:c + width] = val.astype(ref.dtype).reshape(nb, t, width)

    def rope(xb):
        x3 = xb.reshape(nb, t, LANES)
        up = pltpu.roll(xb, LANES - 8, 1).reshape(nb, t, LANES)
        dn = pltpu.roll(xb, 8, 1).reshape(nb, t, LANES)
        r = x3 * cos_ref[...][None] + up * sa_ref[...][None] + dn * sb_ref[...][None]
        return r.reshape(rows, LANES)

    w_mq, w_mk, w_mv, w_mo, w_dq, w_dk, w_dv = widths
    c0 = 0
    for c in range(0, w_mq, MXU_N):
        put(mq_ref, c, mm(c0 + c, min(MXU_N, w_mq - c)))
    c0 += w_mq
    for c in range(0, w_mk, MXU_N):
        put(mk_ref, c, mm(c0 + c, min(MXU_N, w_mk - c)) * k_scale)
    c0 += w_mk
    for c in range(0, w_mv, MXU_N):
        if v_transposed:
            mv_ref[0, c:c + MXU_N, :] = _dot_nt(wv_ref[c:c + MXU_N, :], h_ref[...]).astype(mv_ref.dtype)
        else:
            put(mv_ref, c, _dot(h_ref[...], wv_ref[:, c:c + MXU_N]))
    for c in range(0, w_mo, MXU_N):
        put(og_ref, c, jax.nn.sigmoid(mm(c0 + c, min(MXU_N, w_mo - c))))
    c0 += w_mo
    for c in range(0, w_dq, LANES):
        put(dq_ref, c, rope(mm(c0 + c, LANES)) * q_scale)
    c0 += w_dq
    for c in range(0, w_dk, LANES):
        put(dk_ref, c, rope(mm(c0 + c, LANES)))
    c0 += w_dk
    for c in range(0, w_dv, MXU_N):
        put(dv_ref, c, mm(c0 + c, min(MXU_N, w_dv - c)))
    c0 += w_dv
    z = mm(c0, LANES) + bias_ref[...]
    log_sig = jnp.minimum(z, 0.0) - jnp.log1p(jnp.exp(-jnp.abs(z)))
    lane = lax.broadcasted_iota(jnp.int32, z.shape, 1)
    put(gt_ref, 0, jnp.where(lane < hm, z, jnp.where(lane < 2 * hm, log_sig, 0.0)))


def _project(x, mod, g_norm, w_proj, w_mv, bias, tables, *, nb, t, hm, widths, k_scale, q_scale,
             v_transposed):
    b, s, d = x.shape
    cos_t, sa_t, sb_t = tables
    grid = (b // nb, s // t)
    tok = lambda width: pl.BlockSpec((nb, t, width), lambda i, j: (i, j, 0))
    tab = pl.BlockSpec((t, LANES), lambda i, j: (j, 0))
    w_mq, w_mk, w_mv_, w_mo, w_dq, w_dk, w_dv = widths
    out_widths = [(w_mq, BF16), (w_mk, BF16), (w_mv_, BF16), (w_mo, BF16),
                  (w_dq, BF16), (w_dk, F32), (w_dv, F32), (LANES, F32)]
    out_specs = [tok(w) for w, _ in out_widths]
    out_shape = [jax.ShapeDtypeStruct((b, s, w), dt) for w, dt in out_widths]
    if v_transposed:
        assert nb == 1
        out_specs[2] = pl.BlockSpec((1, w_mv_, t), lambda i, j: (i, 0, j))
        out_shape[2] = jax.ShapeDtypeStruct((b, w_mv_, s), BF16)
    return pl.pallas_call(
        functools.partial(_proj_kernel, nb=nb, t=t, hm=hm, widths=widths,
                          k_scale=k_scale, q_scale=q_scale, v_transposed=v_transposed),
        grid=grid,
        in_specs=[tok(d), pl.BlockSpec((nb, N_ADA, d), lambda i, j: (i, 0, 0)),
                  _resident(g_norm.shape), _resident(w_proj.shape), _resident(w_mv.shape),
                  _resident(bias.shape), tab, tab, tab],
        out_specs=out_specs,
        out_shape=out_shape,
        scratch_shapes=[pltpu.VMEM((nb * t, d), BF16)],
        compiler_params=_params("parallel", "parallel"),
        name="in_proj",
    )(x, mod, g_norm, w_proj, w_mv, bias, cos_t, sa_t, sb_t)


def _mlstm_kernel(q_ref, k_ref, v_ref, og_ref, gt_ref, gm_ref, c0_ref, n0_ref, m0_ref,
                  y_ref, c_ref, n_ref, m_ref, *, L, hm, dk, dv):
    @pl.when(pl.program_id(1) == 0)
    def _():
        c_ref[...] = c0_ref[...]
        n_ref[...] = n0_ref[...]
        m_ref[...] = m0_ref[...]

    gates = gt_ref[0]
    row = lax.broadcasted_iota(jnp.int32, (L, L), 0)
    col = lax.broadcasted_iota(jnp.int32, (L, L), 1)
    causal = col <= row
    diag = col == row

    def as_row(col_vec):
        return jnp.sum(jnp.where(diag, col_vec, 0.0), axis=0, keepdims=True)

    for h in range(hm):
        ig_col = gates[:, h:h + 1]
        lf_row = as_row(gates[:, hm + h:hm + h + 1])
        b_col = jnp.sum(jnp.where(causal, lf_row, 0.0), axis=-1, keepdims=True)
        r_row = as_row(ig_col - b_col)
        m_prev = m_ref[0, h:h + 1, :]
        c_h = c_ref[0, h]
        n_h = n_ref[0, h:h + 1, :]
        q_h = q_ref[0, :, h * dk:(h + 1) * dk]
        k_h = k_ref[0, :, h * dk:(h + 1) * dk]
        v_h = v_ref[0, :, h * dv:(h + 1) * dv]

        dmat = jnp.where(causal, b_col + r_row, -jnp.inf)
        inter = b_col + m_prev
        m_t = jnp.maximum(inter, jnp.max(dmat, axis=-1, keepdims=True))
        w_inter = jnp.exp(inter - m_t)
        s = _dot_nt(q_h, k_h) * jnp.exp(dmat - m_t)
        num = _dot(s.astype(BF16), v_h) + _dot_nt(q_h, c_h.astype(BF16)) * w_inter
        qn = jnp.sum(q_h.astype(F32) * n_h, axis=-1, keepdims=True)
        den = jnp.sum(s, axis=-1, keepdims=True) + w_inter * qn
        den = jnp.maximum(jnp.abs(den), jnp.exp(-m_t))
        hid = num / den
        y = _rms(hid) * gm_ref[h:h + 1, :] * og_ref[0, :, h * dv:(h + 1) * dv].astype(F32)
        y_ref[0, :, h * dv:(h + 1) * dv] = y.astype(y_ref.dtype)

        m_last = m_t[L - 1:L, :]
        b_last = b_col[L - 1:L, :]
        w_state = jnp.exp(b_last + m_prev - m_last)
        w_rows = jnp.exp(b_last - b_col + ig_col - m_last)
        kw = k_h.astype(F32) * w_rows
        c_ref[0, h] = w_state * c_h + _dot_tn(v_h, kw.astype(BF16))
        n_ref[0, h:h + 1, :] = w_state * n_h + jnp.sum(kw, axis=0, keepdims=True)
        m_ref[0, h:h + 1, :] = m_last


def _mlstm_t_kernel(q_ref, k_ref, vt_ref, og_ref, gt_ref, gmt_ref, y_ref, c_ref, n_ref, m_ref, st_ref,
                    *, L, hm, dk, dv):
    ci = pl.program_id(1)

    @pl.when(ci == 0)
    def _():
        st_ref[...] = jnp.zeros_like(st_ref)
        m_ref[...] = jnp.zeros_like(m_ref)

    gates = gt_ref[0]
    src = lax.broadcasted_iota(jnp.int32, (L, L), 0)
    tgt = lax.broadcasted_iota(jnp.int32, (L, L), 1)
    visible = src <= tgt
    lower = (tgt <= src).astype(BF16)
    g1 = gates.astype(BF16)
    r1 = gates - g1.astype(F32)
    g2 = r1.astype(BF16)
    g3 = (r1 - g2.astype(F32)).astype(BF16)
    csum = _dot(lower, g1) + _dot(lower, g2) + _dot(lower, g3)
    csum_t = csum.T
    ones_rows = (lax.broadcasted_iota(jnp.int32, (BF16_ROWS, L), 0) == 0).astype(BF16)

    for h in range(hm):
        r_col = gates[:, h:h + 1] - csum[:, hm + h:hm + h + 1]
        b_row = csum_t[hm + h:hm + h + 1, :]
        m_prev = m_ref[0, h:h + 1, :]
        state = st_ref[h]
        q_h = q_ref[0, :, h * dk:(h + 1) * dk]
        k_h = k_ref[0, :, h * dk:(h + 1) * dk]
        vt_h = vt_ref[0, h * dv:(h + 1) * dv, :]

        dmat = jnp.where(visible, r_col + b_row, -jnp.inf)
        inter = b_row + m_prev
        m_t = jnp.maximum(inter, jnp.max(dmat, axis=0, keepdims=True))
        w_inter = jnp.exp(inter - m_t)
        s = _dot_nt(k_h, q_h) * jnp.exp(dmat - m_t)
        sq = _dot_nt(state.astype(BF16), q_h)
        num = _dot(vt_h, s.astype(BF16)) + sq[:dv, :] * w_inter
        den = jnp.sum(s, axis=0, keepdims=True) + w_inter * sq[dv:dv + 1, :]
        den = jnp.maximum(jnp.abs(den), jnp.exp(-m_t))
        hid = num / den
        scale = lax.rsqrt(jnp.mean(hid * hid, axis=0, keepdims=True) + NORM_EPS)
        y_t = hid * scale * gmt_ref[:, h:h + 1]
        y = y_t.T * og_ref[0, :, h * dv:(h + 1) * dv].astype(F32)
        y_ref[0, :, h * dv:(h + 1) * dv] = y.astype(y_ref.dtype)

        m_last = m_t[:, L - 1:L]
        b_last = b_row[:, L - 1:L]
        w_state = jnp.exp(b_last + m_prev - m_last)
        w_rows = jnp.exp(r_col + (b_last - m_last))
        kw = (k_h.astype(F32) * w_rows).astype(BF16)
        vt_aug = jnp.concatenate([vt_h, ones_rows], axis=0)
        st_ref[h] = w_state * state + _dot(vt_aug, kw)
        m_ref[0, h:h + 1, :] = m_last

    @pl.when(ci == pl.num_programs(1) - 1)
    def _():
        for h in range(hm):
            c_ref[0, h] = st_ref[h, :dv, :]
            n_ref[0, h:h + 1, :] = st_ref[h, dv:dv + 1, :]


def _mlstm(mq, mk, mv, og, gates, g_mlstm, *, L, init=None):
    b, s, _ = mq.shape
    hm, dv = g_mlstm.shape
    dk = mq.shape[-1] // hm
    tok = lambda width: pl.BlockSpec((1, L, width), lambda i, j: (i, j, 0))
    st_c = pl.BlockSpec((1, hm, dv, dk), lambda i, j: (i, 0, 0, 0))
    st_n = pl.BlockSpec((1, hm, dk), lambda i, j: (i, 0, 0))
    st_m = pl.BlockSpec((1, hm, 1), lambda i, j: (i, 0, 0))
    out_specs = [tok(hm * dv), st_c, st_n, st_m]
    out_shape = [jax.ShapeDtypeStruct((b, s, hm * dv), BF16),
                 jax.ShapeDtypeStruct((b, hm, dv, dk), F32),
                 jax.ShapeDtypeStruct((b, hm, dk), F32),
                 jax.ShapeDtypeStruct((b, hm, 1), F32)]
    if init is not None:
        return pl.pallas_call(
            functools.partial(_mlstm_kernel, L=L, hm=hm, dk=dk, dv=dv),
            grid=(b, s // L),
            in_specs=[tok(hm * dk), tok(hm * dk), tok(hm * dv), tok(hm * dv), tok(LANES),
                      _resident(g_mlstm.shape), st_c, st_n, st_m],
            out_specs=out_specs,
            out_shape=out_shape,
            compiler_params=_params("parallel", "arbitrary"),
            name="mlstm_init",
        )(mq, mk, mv, og, gates, g_mlstm, *init)
    g_t = g_mlstm.T
    return pl.pallas_call(
        functools.partial(_mlstm_t_kernel, L=L, hm=hm, dk=dk, dv=dv),
        grid=(b, s // L),
        in_specs=[tok(hm * dk), tok(hm * dk), pl.BlockSpec((1, hm * dv, L), lambda i, j: (i, 0, j)),
                  tok(hm * dv), tok(LANES), _resident(g_t.shape)],
        out_specs=out_specs,
        out_shape=out_shape,
        scratch_shapes=[pltpu.VMEM((hm, dv + BF16_ROWS, dk), F32)],
        compiler_params=_params("parallel", "arbitrary"),
        name="mlstm",
    )(mq, mk, mv, og, gates, g_t)


def _diff_lambda(lam_ref, layer):
    lq = lam_ref[...]
    a = jnp.sum(lq[0:1, :] * lq[1:2, :], axis=-1, keepdims=True)
    b = jnp.sum(lq[2:3, :] * lq[3:4, :], axis=-1, keepdims=True)
    lam_init = 0.8 - 0.6 * float(np.exp(-0.3 * layer))
    return jnp.exp(a) - jnp.exp(b) + lam_init, lam_init


def _split_maps(q):
    lane = lax.broadcasted_iota(jnp.int32, q.shape, 1)
    zero = jnp.zeros_like(q)
    half = q.shape[-1] // 2
    return jnp.where(lane < half, q, zero), jnp.where(lane >= half, q, zero)


def _attn_prompt_kernel(q_ref, k_ref, v_ref, lam_ref, g_ref, o_ref, kb_ref, vt_ref, s_ref, p_ref,
                        *, tq, layer):
    s_len, dv = k_ref.shape[1], k_ref.shape[2]
    n_blk = s_len // tq
    for j in range(n_blk):
        kb_ref[j] = k_ref[0, j * tq:(j + 1) * tq, :].astype(BF16)
        vt_ref[:, j * tq:(j + 1) * tq] = v_ref[0, j * tq:(j + 1) * tq, :].T.astype(BF16)

    lam, lam_init = _diff_lambda(lam_ref, layer)
    key_chunk = lax.broadcasted_iota(jnp.int32, (tq, tq), 0) // CHUNK
    qry_chunk = lax.broadcasted_iota(jnp.int32, (tq, tq), 1) // CHUNK
    visible = key_chunk <= qry_chunk

    for qi in range(n_blk):
        slot = qi % 2
        n_keys = (qi + 1) * tq
        q_maps = _split_maps(q_ref[0, qi * tq:(qi + 1) * tq, :])
        outs = []
        for c in range(2):
            m = None
            for j in range(qi + 1):
                s = _dot_nt(kb_ref[j], q_maps[c])
                if j == qi:
                    s = jnp.where(visible, s, -jnp.inf)
                s_ref[slot, c, j] = s
                m_j = jnp.max(s, axis=0, keepdims=True)
                m = m_j if m is None else jnp.maximum(m, m_j)
            l = None
            for j in range(qi + 1):
                p = jnp.exp(s_ref[slot, c, j] - m)
                l_j = jnp.sum(p, axis=0, keepdims=True)
                l = l_j if l is None else l + l_j
                p_ref[slot, c, j * tq:(j + 1) * tq, :] = p.astype(BF16)
            acc = _dot(vt_ref[:, :n_keys], p_ref[slot, c, :n_keys, :])
            outs.append(acc / l)
        o = (outs[0] - lam * outs[1]).T
        o_ref[0, qi * tq:(qi + 1) * tq, :] = (_rms(o) * g_ref[0] * (1.0 - lam_init)).astype(o_ref.dtype)


def _attn_prompt(dq, dk, dv, lambda_qk, g_diff, *, tq, layer):
    b, s, _ = dq.shape
    hd, dvh = g_diff.shape
    n_blk = s // tq
    head = pl.BlockSpec((1, s, dvh), lambda i, h: (i, 0, h))
    return pl.pallas_call(
        functools.partial(_attn_prompt_kernel, tq=tq, layer=layer),
        grid=(b, hd),
        in_specs=[head, head, head, _resident(lambda_qk.shape),
                  pl.BlockSpec((1, 1, dvh), lambda i, h: (h, 0, 0))],
        out_specs=head,
        out_shape=jax.ShapeDtypeStruct((b, s, hd * dvh), BF16),
        scratch_shapes=[pltpu.VMEM((n_blk, tq, dvh), BF16), pltpu.VMEM((dvh, s), BF16),
                        pltpu.VMEM((2, 2, n_blk, tq, tq), F32), pltpu.VMEM((2, 2, s, tq), BF16)],
        compiler_params=_params("parallel", "parallel"),
        name="diff_attn_prompt",
    )(dq, dk, dv, lambda_qk, g_diff.reshape(hd, 1, dvh))


def _attn_sample_kernel(q_ref, kc_ref, vc_ref, kn_ref, vn_ref, lam_ref, g_ref, o_ref, *, layer):
    lam, lam_init = _diff_lambda(lam_ref, layer)
    q_maps = _split_maps(q_ref[0])
    k_c = kc_ref[0].astype(BF16)
    k_n = kn_ref[0].astype(BF16)
    v_c = vc_ref[0].astype(BF16)
    v_n = vn_ref[0].astype(BF16)
    outs = []
    for c in range(2):
        s_c = _dot_nt(q_maps[c], k_c)
        s_n = _dot_nt(q_maps[c], k_n)
        m = jnp.maximum(jnp.max(s_c, axis=-1, keepdims=True), jnp.max(s_n, axis=-1, keepdims=True))
        p_c = jnp.exp(s_c - m)
        p_n = jnp.exp(s_n - m)
        l = jnp.sum(p_c, axis=-1, keepdims=True) + jnp.sum(p_n, axis=-1, keepdims=True)
        outs.append((_dot(p_c.astype(BF16), v_c) + _dot(p_n.astype(BF16), v_n)) / l)
    o = outs[0] - lam * outs[1]
    o_ref[0] = (_rms(o) * g_ref[0] * (1.0 - lam_init)).astype(o_ref.dtype)


def _attn_sample(dq, dk, dv, cache_k, cache_v, lambda_qk, g_diff, *, layer):
    b, t, _ = dq.shape
    p = cache_k.shape[1]
    hd, dvh = g_diff.shape
    new = pl.BlockSpec((1, t, dvh), lambda i, h: (i, 0, h))
    old = pl.BlockSpec((1, p, dvh), lambda i, h: (i, 0, h))
    return pl.pallas_call(
        functools.partial(_attn_sample_kernel, layer=layer),
        grid=(b, hd),
        in_specs=[new, old, old, new, new, _resident(lambda_qk.shape),
                  pl.BlockSpec((1, 1, dvh), lambda i, h: (h, 0, 0))],
        out_specs=new,
        out_shape=jax.ShapeDtypeStruct((b, t, hd * dvh), BF16),
        compiler_params=_params("parallel", "parallel"),
        name="diff_attn_sample",
    )(dq, cache_k, cache_v, dk, dv, lambda_qk, g_diff.reshape(hd, 1, dvh))


def _interleave_gate_up(w_in):
    d, two_f = w_in.shape
    n = two_f // 2 // LANES
    return w_in.reshape(d, 2, n, LANES).transpose(0, 2, 1, 3).reshape(d, two_f).astype(BF16)


def _rope_tables(pos, dqk):
    rope_dim = dqk // 4
    half = rope_dim // 2
    inv_freq = ROPE_THETA ** (-jnp.arange(half, dtype=F32) * (2.0 / rope_dim))
    ang = pos.astype(F32)[:, None] * inv_freq[None, :]
    cos, sin = jnp.cos(ang), jnp.sin(ang)
    n = pos.shape[0]
    pad = jnp.zeros((n, dqk - rope_dim), F32)
    zero = jnp.zeros((n, half), F32)
    cos_t = jnp.concatenate([cos, cos, pad + 1.0], axis=1)
    sa_t = jnp.concatenate([-sin, zero, pad], axis=1)
    sb_t = jnp.concatenate([zero, sin, pad], axis=1)
    rep = LANES // dqk
    return tuple(jnp.tile(a, (1, rep)) for a in (cos_t, sa_t, sb_t))


def _layer(x, c_mod, pos, attend, mlstm_init, weights, dims, *, nb, t, L, layer):
    (g_norm, w1_in, w1_out, w_proj, w_mv, bias, g_mlstm, g_diff, lambda_qk, w_o, w2_in, w2_out) = weights
    hm, dk_m, widths = dims
    v_transposed = mlstm_init is None
    x1 = _ffn_block(x, c_mod, g_norm, w1_in, w1_out, nb=nb, t=t, g_idx=(0, 1), m_idx=(0, 1, 2))
    dqk = g_diff.shape[1] // 2
    mq, mk, mv, og, dq, dk, dv, gates = _project(
        x1, c_mod, g_norm, w_proj, w_mv.T if v_transposed else w_mv, bias, _rope_tables(pos, dqk),
        nb=nb, t=t, hm=hm, widths=widths, k_scale=dk_m ** -0.5, q_scale=dqk ** -0.5,
        v_transposed=v_transposed)
    y_m, c_new, n_new, m_new = _mlstm(mq, mk, mv, og, gates, g_mlstm, L=L, init=mlstm_init)
    y_d = attend(dq, dk, dv)
    y = _ffn_block(x1, c_mod, g_norm, w2_in, w2_out, nb=nb, t=t, g_idx=(4, 5), m_idx=(6, 7, 8),
                   heads=(y_m, y_d), w_o=w_o)
    return y, (dk, dv, c_new, n_new, m_new)


def kernel(x_prompt, x_sample, cache_k, cache_v, state_C, state_n, state_m, c_prompt, c_sample,
           w_ada, b_ada, g_norm, w_ffn1_in, w_ffn1_out, w_in, b_igate, b_fgate, g_mlstm, g_diff,
           lambda_qk, w_out, w_ffn2_in, w_ffn2_out):
    depth = w_ada.shape[0]
    bp, sp, d = x_prompt.shape
    bs, ts, _ = x_sample.shape
    past = cache_k.shape[2]
    hm, mv_dim = g_mlstm.shape[1:]
    hd, dv_dim = g_diff.shape[1:]
    mqk = mv_dim // 2
    dqk = dv_dim // 2
    sizes = [hm * mqk, hm * mqk, hm * mv_dim, hm * mv_dim, hm, hm, hd * 2 * dqk, hd * 2 * dqk, hd * dv_dim]
    offs = np.concatenate([[0], np.cumsum(sizes)])
    widths = tuple(sizes[i] for i in (0, 1, 2, 3, 6, 7, 8))
    t_prompt = min(512, sp)
    l_prompt = min(256, sp)

    y_p, y_s = x_prompt, x_sample
    states_p, states_s = [], []
    for l in range(depth):
        mod = _ada(jnp.concatenate([c_prompt, c_sample], axis=0), w_ada[l], b_ada[l])
        mod = mod.reshape(bp + bs, N_ADA, d)
        cols = [w_in[l][:, offs[i]:offs[i + 1]] for i in range(9)]
        gate_cols = jnp.concatenate([cols[4], cols[5], jnp.zeros((d, LANES - 2 * hm), F32)], axis=1)
        w_proj = jnp.concatenate([cols[i] for i in (0, 1, 3, 6, 7, 8)] + [gate_cols], axis=1).astype(BF16)
        bias = jnp.concatenate([b_igate[l], b_fgate[l], jnp.zeros((LANES - 2 * hm,), F32)]).reshape(1, LANES)
        weights = (g_norm[l], _interleave_gate_up(w_ffn1_in[l]), w_ffn1_out[l].astype(BF16), w_proj,
                   cols[2].astype(BF16), bias, g_mlstm[l], g_diff[l], lambda_qk[l], w_out[l].astype(BF16),
                   _interleave_gate_up(w_ffn2_in[l]), w_ffn2_out[l].astype(BF16))
        dims = (hm, mqk, widths)

        attend_p = functools.partial(_attn_prompt, lambda_qk=lambda_qk[l], g_diff=g_diff[l],
                                     tq=min(256, sp), layer=l)
        y_p, st_p = _layer(y_p, mod[:bp], jnp.arange(sp), attend_p, None, weights, dims,
                           nb=1, t=t_prompt, L=l_prompt, layer=l)

        ck = cache_k[l].reshape(bs, past, hd * 2 * dqk)
        cv = cache_v[l].reshape(bs, past, hd * dv_dim)
        attend_s = functools.partial(_attn_sample, cache_k=ck, cache_v=cv, lambda_qk=lambda_qk[l],
                                     g_diff=g_diff[l], layer=l)
        init_s = (state_C[l], state_n[l], state_m[l].reshape(bs, hm, 1))
        y_s, st_s = _layer(y_s, mod[bp:], past + jnp.arange(ts), attend_s, init_s, weights, dims,
                           nb=bs, t=ts, L=ts, layer=l)
        states_p.append(st_p)
        states_s.append(st_s)

    def stack(states, b, s):
        if depth == 1:
            k, v, c, n, m = [a[None] for a in states[0]]
        else:
            k, v, c, n, m = [jnp.stack(a, axis=0) for a in zip(*states)]
        return (k.reshape(depth, b, s, hd, 2 * dqk), v.reshape(depth, b, s, hd, dv_dim),
                c, n, m.reshape(depth, b, hm))

    return (y_p, y_s) + stack(states_p, bp, sp) + stack(states_s, bs, ts)
```

```python
import functools

import jax
import jax.numpy as jnp
import numpy as np
from jax import lax
from jax.experimental import pallas as pl
from jax.experimental.pallas import tpu as pltpu

F32 = jnp.float32
BF16 = jnp.bfloat16

LANES = 128
BF16_ROWS = 16
MXU_N = 256
VMEM_LIMIT_BYTES = 56 << 20
NORM_EPS = 1e-6
ROPE_THETA = 500000.0
N_ADA = 9
CHUNK = 64
LOG2_E = 1.4426950408889634

_NT = (((1,), (1,)), ((), ()))
_TN = (((0,), (0,)), ((), ()))


def _dot(a, b):
    return jnp.dot(a, b, preferred_element_type=F32)


def _dot_nt(a, b):
    return lax.dot_general(a, b, _NT, preferred_element_type=F32)


def _dot_tn(a, b):
    return lax.dot_general(a, b, _TN, preferred_element_type=F32)


def _rms(x):
    return x * lax.rsqrt(jnp.mean(x * x, axis=-1, keepdims=True) + NORM_EPS)


def _silu(x):
    return x * jax.nn.sigmoid(x)


def _params(*sem):
    return pltpu.CompilerParams(dimension_semantics=sem, vmem_limit_bytes=VMEM_LIMIT_BYTES)


def _resident(shape):
    nd = len(shape)
    return pl.BlockSpec(shape, lambda *_: (0,) * nd, pipeline_mode=pl.Buffered(1))


def _ada_kernel(c_ref, w_ref, b_ref, o_ref):
    a = _silu(c_ref[...]).astype(BF16)
    o_ref[...] = _dot(a, w_ref[...].astype(BF16)) + b_ref[...]


def _ada(c, w_ada, b_ada):
    n, d = c.shape
    width = w_ada.shape[1]
    tn = width // 8
    return pl.pallas_call(
        _ada_kernel,
        grid=(width // tn,),
        in_specs=[pl.BlockSpec((n, d), lambda j: (0, 0)),
                  pl.BlockSpec((d, tn), lambda j: (0, j)),
                  pl.BlockSpec((1, tn), lambda j: (0, j))],
        out_specs=pl.BlockSpec((n, tn), lambda j: (0, j)),
        out_shape=jax.ShapeDtypeStruct((n, width), F32),
        compiler_params=_params("parallel"),
        name="ada",
    )(c, w_ada, b_ada.reshape(1, width))


def _ffn_kernel(*refs, nb, t, pre_proj, ym_transposed, g_idx, m_idx):
    if pre_proj:
        x_ref, ym_ref, yd_ref, mod_ref, g_ref, wo_ref, w_in_ref, w_out_ref, o_ref, h_ref, act_ref = refs
    else:
        x_ref, mod_ref, g_ref, w_in_ref, w_out_ref, o_ref, h_ref, act_ref = refs
    d = x_ref.shape[-1]
    rows = nb * t
    x = x_ref[...]

    def mrow(i):
        return mod_ref[:, i:i + 1, :]

    def grow(i):
        return g_ref[i:i + 1, :][None]

    if pre_proj:
        if ym_transposed:
            w_m = ym_ref.shape[1]
            o = _dot_tn(ym_ref[0], wo_ref[:w_m, :]) + _dot(yd_ref[0], wo_ref[w_m:, :])
        else:
            heads = jnp.concatenate([ym_ref[...], yd_ref[...]], axis=-1).reshape(rows, d)
            o = _dot(heads, wo_ref[...])
        o = o.reshape(nb, t, d)
        x = x + mrow(5) * (_rms(o) * grow(3))

    g_a, g_b = g_idx
    i_shift, i_scale, i_gate = m_idx
    h = _rms(x) * grow(g_a) * (1.0 + mrow(i_scale)) + mrow(i_shift)
    h_ref[...] = h.astype(BF16).reshape(rows, d)

    n_groups = w_out_ref.shape[0] // LANES
    for j in range(n_groups):
        gu = _dot(h_ref[...], w_in_ref[:, j * MXU_N:(j + 1) * MXU_N])
        act = _silu(gu[:, :LANES]) * gu[:, LANES:]
        act_ref[:, j * LANES:(j + 1) * LANES] = act.astype(BF16)
    out = _dot(act_ref[...], w_out_ref[...]).reshape(nb, t, d)
    o_ref[...] = x + (0.5 * mrow(i_gate)) * (_rms(out) * grow(g_b))


def _ffn_block(x, mod, g_norm, w_in_r, w_out, *, nb, t, g_idx, m_idx, heads=None, w_o=None,
               ym_transposed=False):
    b, s, d = x.shape
    f = w_out.shape[0]
    pre_proj = heads is not None
    grid = (b // nb, s // t)
    tok = lambda width: pl.BlockSpec((nb, t, width), lambda i, j: (i, j, 0))
    in_specs = [tok(d)]
    args = [x]
    if pre_proj:
        ym, yd = heads
        if ym_transposed:
            assert nb == 1
            ym_spec = pl.BlockSpec((1, ym.shape[1], t), lambda i, j: (i, 0, j))
        else:
            ym_spec = tok(ym.shape[-1])
        in_specs += [ym_spec, tok(yd.shape[-1])]
        args += [ym, yd]
    in_specs += [pl.BlockSpec((nb, N_ADA, d), lambda i, j: (i, 0, 0)), _resident(g_norm.shape)]
    args += [mod, g_norm]
    if pre_proj:
        in_specs.append(_resident(w_o.shape))
        args.append(w_o)
    in_specs += [_resident(w_in_r.shape), _resident(w_out.shape)]
    args += [w_in_r, w_out]
    return pl.pallas_call(
        functools.partial(_ffn_kernel, nb=nb, t=t, pre_proj=pre_proj, ym_transposed=ym_transposed,
                          g_idx=g_idx, m_idx=m_idx),
        grid=grid,
        in_specs=in_specs,
        out_specs=tok(d),
        out_shape=jax.ShapeDtypeStruct((b, s, d), F32),
        scratch_shapes=[pltpu.VMEM((nb * t, d), BF16), pltpu.VMEM((nb * t, f), BF16)],
        compiler_params=_params("parallel", "parallel"),
        name="ffn_post" if pre_proj else "ffn_pre",
    )(*args)


def _proj_kernel(x_ref, mod_ref, g_ref, w_ref, wv_ref, bias_ref, cos_ref, sa_ref, sb_ref,
                 mq_ref, mk_ref, mv_ref, og_ref, dq_ref, dk_ref, dv_ref, kst_ref, vst_ref, gt_ref, h_ref,
                 *, nb, t, hm, widths, k_scale, q_scale, v_transposed):
    d = x_ref.shape[-1]
    rows = nb * t
    h = _rms(x_ref[...]) * g_ref[2:3, :][None] * (1.0 + mod_ref[:, 4:5, :]) + mod_ref[:, 3:4, :]
    h_ref[...] = h.astype(BF16).reshape(rows, d)

    def mm(c0, width):
        return _dot(h_ref[...], w_ref[:, c0:c0 + width])

    def put(ref, c, val):
        width = val.shape[-1]
        ref[:, :, c:c + width] = val.astype(ref.dtype).reshape(nb, t, width)

    def rope(xb):
        x3 = xb.reshape(nb, t, LANES)
        up = pltpu.roll(xb, LANES - 8, 1).reshape(nb, t, LANES)
        dn = pltpu.roll(xb, 8, 1).reshape(nb, t, LANES)
        r = x3 * cos_ref[...][None] + up * sa_ref[...][None] + dn * sb_ref[...][None]
        return r.reshape(rows, LANES)

    w_mq, w_mk, w_mv, w_mo, w_dq, w_dk, w_dv = widths
    c0 = 0
    for c in range(0, w_mq, MXU_N):
        put(mq_ref, c, mm(c0 + c, min(MXU_N, w_mq - c)))
    c0 += w_mq
    for c in range(0, w_mk, MXU_N):
        put(mk_ref, c, mm(c0 + c, min(MXU_N, w_mk - c)) * k_scale)
    c0 += w_mk
    for c in range(0, w_mv, MXU_N):
        if v_transposed:
            mv_ref[0, c:c + MXU_N, :] = _dot_nt(wv_ref[c:c + MXU_N, :], h_ref[...]).astype(mv_ref.dtype)
        else:
            put(mv_ref, c, _dot(h_ref[...], wv_ref[:, c:c + MXU_N]))
    for c in range(0, w_mo, MXU_N):
        if v_transposed:
            pre = _dot_nt(wv_ref[w_mv + c:w_mv + c + MXU_N, :], h_ref[...])
            og_ref[0, c:c + MXU_N, :] = jax.nn.sigmoid(pre).astype(og_ref.dtype)
        else:
            put(og_ref, c, jax.nn.sigmoid(_dot(h_ref[...], wv_ref[:, w_mv + c:w_mv + c + MXU_N])))
    n_heads = w_dk // LANES

    def put_state(ref, c, val):
        ref[:, pl.ds(c // LANES, t, stride=n_heads), :] = val.reshape(nb, t, LANES)

    for c in range(0, w_dq, MXU_N):
        r = mm(c0 + c, MXU_N)
        for i in range(0, MXU_N, LANES):
            put(dq_ref, c + i, rope(r[:, i:i + LANES]) * q_scale)
    c0 += w_dq
    for c in range(0, w_dk, MXU_N):
        r = mm(c0 + c, MXU_N)
        for i in range(0, MXU_N, LANES):
            k_rot = rope(r[:, i:i + LANES])
            put(dk_ref, c + i, k_rot)
            put_state(kst_ref, c + i, k_rot)
    c0 += w_dk
    for c in range(0, w_dv, MXU_N):
        r = mm(c0 + c, MXU_N)
        put(dv_ref, c, r)
        for i in range(0, MXU_N, LANES):
            put_state(vst_ref, c + i, r[:, i:i + LANES])
    c0 += w_dv
    z = mm(c0, LANES) + bias_ref[...]
    log_sig = jnp.minimum(z, 0.0) - jnp.log1p(jnp.exp(-jnp.abs(z)))
    lane = lax.broadcasted_iota(jnp.int32, z.shape, 1)
    put(gt_ref, 0, jnp.where(lane < hm, z, jnp.where(lane < 2 * hm, log_sig, 0.0)))


def _project(x, mod, g_norm, w_proj, w_mv, bias, tables, *, nb, t, hm, widths, k_scale, q_scale,
             v_transposed):
    b, s, d = x.shape
    cos_t, sa_t, sb_t = tables
    grid = (b // nb, s // t)
    tok = lambda width: pl.BlockSpec((nb, t, width), lambda i, j: (i, j, 0))
    tab = pl.BlockSpec((t, LANES), lambda i, j: (j, 0))
    w_mq, w_mk, w_mv_, w_mo, w_dq, w_dk, w_dv = widths
    n_heads = w_dk // LANES
    out_widths = [(w_mq, BF16), (w_mk, BF16), (w_mv_, BF16), (w_mo, BF16),
                  (w_dq, BF16), (w_dk, BF16), (w_dv, BF16), (0, F32), (0, F32), (LANES, F32)]
    out_specs = [tok(w) for w, _ in out_widths]
    out_shape = [jax.ShapeDtypeStruct((b, s, w), dt) for w, dt in out_widths]
    for i in (7, 8):
        out_specs[i] = pl.BlockSpec((nb, t * n_heads, LANES), lambda i, j: (i, j, 0))
        out_shape[i] = jax.ShapeDtypeStruct((b, s * n_heads, LANES), F32)
    if v_transposed:
        assert nb == 1
        for i, w in ((2, w_mv_), (3, w_mo)):
            out_specs[i] = pl.BlockSpec((1, w, t), lambda i, j: (i, 0, j))
            out_shape[i] = jax.ShapeDtypeStruct((b, w, s), BF16)
    return pl.pallas_call(
        functools.partial(_proj_kernel, nb=nb, t=t, hm=hm, widths=widths,
                          k_scale=k_scale, q_scale=q_scale, v_transposed=v_transposed),
        grid=grid,
        in_specs=[tok(d), pl.BlockSpec((nb, N_ADA, d), lambda i, j: (i, 0, 0)),
                  _resident(g_norm.shape), _resident(w_proj.shape), _resident(w_mv.shape),
                  _resident(bias.shape), tab, tab, tab],
        out_specs=out_specs,
        out_shape=out_shape,
        scratch_shapes=[pltpu.VMEM((nb * t, d), BF16)],
        compiler_params=_params("parallel", "parallel"),
        name="in_proj",
    )(x, mod, g_norm, w_proj, w_mv, bias, cos_t, sa_t, sb_t)


def _mlstm_kernel(q_ref, k_ref, v_ref, og_ref, gt_ref, gm_ref, c0_ref, n0_ref, m0_ref,
                  y_ref, c_ref, n_ref, m_ref, *, L, hm, dk, dv):
    @pl.when(pl.program_id(1) == 0)
    def _():
        c_ref[...] = c0_ref[...]
        n_ref[...] = n0_ref[...]
        m_ref[...] = m0_ref[...]

    gates = gt_ref[0]
    row = lax.broadcasted_iota(jnp.int32, (L, L), 0)
    col = lax.broadcasted_iota(jnp.int32, (L, L), 1)
    causal = col <= row
    diag = col == row

    def as_row(col_vec):
        return jnp.sum(jnp.where(diag, col_vec, 0.0), axis=0, keepdims=True)

    for h in range(hm):
        ig_col = gates[:, h:h + 1]
        lf_row = as_row(gates[:, hm + h:hm + h + 1])
        b_col = jnp.sum(jnp.where(causal, lf_row, 0.0), axis=-1, keepdims=True)
        r_row = as_row(ig_col - b_col)
        m_prev = m_ref[0, h:h + 1, :]
        c_h = c_ref[0, h]
        n_h = n_ref[0, h:h + 1, :]
        q_h = q_ref[0, :, h * dk:(h + 1) * dk]
        k_h = k_ref[0, :, h * dk:(h + 1) * dk]
        v_h = v_ref[0, :, h * dv:(h + 1) * dv]

        dmat = jnp.where(causal, b_col + r_row, -jnp.inf)
        inter = b_col + m_prev
        m_t = jnp.maximum(inter, jnp.max(dmat, axis=-1, keepdims=True))
        w_inter = jnp.exp(inter - m_t)
        s = _dot_nt(q_h, k_h) * jnp.exp(dmat - m_t)
        num = _dot(s.astype(BF16), v_h) + _dot_nt(q_h, c_h.astype(BF16)) * w_inter
        qn = jnp.sum(q_h.astype(F32) * n_h, axis=-1, keepdims=True)
        den = jnp.sum(s, axis=-1, keepdims=True) + w_inter * qn
        den = jnp.maximum(jnp.abs(den), jnp.exp(-m_t))
        hid = num / den
        y = _rms(hid) * gm_ref[h:h + 1, :] * og_ref[0, :, h * dv:(h + 1) * dv].astype(F32)
        y_ref[0, :, h * dv:(h + 1) * dv] = y.astype(y_ref.dtype)

        m_last = m_t[L - 1:L, :]
        b_last = b_col[L - 1:L, :]
        w_state = jnp.exp(b_last + m_prev - m_last)
        w_rows = jnp.exp(b_last - b_col + ig_col - m_last)
        kw = k_h.astype(F32) * w_rows
        c_ref[0, h] = w_state * c_h + _dot_tn(v_h, kw.astype(BF16))
        n_ref[0, h:h + 1, :] = w_state * n_h + jnp.sum(kw, axis=0, keepdims=True)
        m_ref[0, h:h + 1, :] = m_last


def _mlstm_t_kernel(q_ref, k_ref, vt_ref, og_ref, gt_ref, gmt_ref, y_ref, c_ref, n_ref, m_ref, st_ref,
                    *, rows, L, hm, dk, dv):
    ci = pl.program_id(1)

    @pl.when(ci == 0)
    def _():
        st_ref[...] = jnp.zeros_like(st_ref)
        m_ref[...] = jnp.zeros_like(m_ref)

    src = lax.broadcasted_iota(jnp.int32, (L, L), 0)
    tgt = lax.broadcasted_iota(jnp.int32, (L, L), 1)
    visible = src <= tgt
    lower = (tgt <= src).astype(BF16)
    ones_rows = (lax.broadcasted_iota(jnp.int32, (BF16_ROWS, L), 0) == 0).astype(BF16)

    gates_all = jnp.concatenate([gt_ref[r] for r in range(rows)], axis=-1)
    g1 = gates_all.astype(BF16)
    r1 = gates_all - g1.astype(F32)
    g2 = r1.astype(BF16)
    g3 = (r1 - g2.astype(F32)).astype(BF16)
    csum_all = _dot(lower, g1) + _dot(lower, g2) + _dot(lower, g3)

    row_vals = {}

    def decay_stage(r, h):
        if r not in row_vals:
            csum = csum_all[:, r * LANES:(r + 1) * LANES]
            row_vals[r] = (gt_ref[r], csum, csum.T)
        gates, csum, csum_t = row_vals[r]
        r_col = gates[:, h:h + 1] - csum[:, hm + h:hm + h + 1]
        r_lanes = jnp.broadcast_to(r_col, (L, LANES))
        b_row = csum_t[hm + h:hm + h + 1, :]
        m_prev = m_ref[r, h:h + 1, :]
        dmat = jnp.where(visible, jnp.concatenate([r_lanes] * (L // LANES), axis=-1) + b_row, -jnp.inf)
        inter = b_row + m_prev
        m_t = jnp.maximum(inter, jnp.max(dmat, axis=0, keepdims=True))
        m_last = m_t[:, L - 1:L]
        b_last = b_row[:, L - 1:L]
        return dict(
            w_intra=jnp.exp(dmat - m_t), w_inter=jnp.exp(inter - m_t), floor=jnp.exp(-m_t),
            w_state=jnp.exp(b_last + m_prev - m_last), w_rows=jnp.exp(r_lanes[:, :dk] + (b_last - m_last)),
            m_last=m_last)

    def matmul_stage(r, h, w):
        state = st_ref[r, h]
        q_h = q_ref[r, :, h * dk:(h + 1) * dk]
        k_h = k_ref[r, :, h * dk:(h + 1) * dk]
        vt_h = vt_ref[r, h * dv:(h + 1) * dv, :]
        s = _dot_nt(k_h, q_h) * w["w_intra"]
        sq = _dot_nt(state.astype(BF16), q_h)
        num = _dot(vt_h, s.astype(BF16)) + sq[:dv, :] * w["w_inter"]
        den = jnp.sum(s, axis=0, keepdims=True) + w["w_inter"] * sq[dv:dv + 1, :]
        hid = num / jnp.maximum(jnp.abs(den), w["floor"])
        scale = lax.rsqrt(jnp.mean(hid * hid, axis=0, keepdims=True) + NORM_EPS)
        gain = jnp.concatenate([gmt_ref[:, h * LANES:(h + 1) * LANES]] * (L // LANES), axis=-1)
        y_t = hid * scale * gain * og_ref[r, h * dv:(h + 1) * dv, :].astype(F32)
        y_ref[r, h * dv:(h + 1) * dv, :] = y_t.astype(y_ref.dtype)

        kw = (k_h.astype(F32) * w["w_rows"]).astype(BF16)
        vt_aug = jnp.concatenate([vt_h, ones_rows], axis=0)
        st_ref[r, h] = w["w_state"] * state + _dot(vt_aug, kw)
        m_ref[r, h:h + 1, :] = w["m_last"]

    chains = [(r, h) for r in range(rows) for h in range(hm)]
    pending = None
    for chain in chains + [None]:
        weights = decay_stage(*chain) if chain is not None else None
        if pending is not None:
            matmul_stage(*pending)
        pending = chain + (weights,) if chain is not None else None

    @pl.when(ci == pl.num_programs(1) - 1)
    def _():
        for r in range(rows):
            for h in range(hm):
                c_ref[r, h] = st_ref[r, h, :dv, :]
                n_ref[r, h:h + 1, :] = st_ref[r, h, dv:dv + 1, :]


def _mlstm(mq, mk, mv, og, gates, g_mlstm, *, L, rows=1, init=None):
    b, s, _ = mq.shape
    hm, dv = g_mlstm.shape
    dk = mq.shape[-1] // hm
    assert init is None or rows == 1
    tok = lambda width: pl.BlockSpec((rows, L, width), lambda i, j: (i, j, 0))
    st_c = pl.BlockSpec((rows, hm, dv, dk), lambda i, j: (i, 0, 0, 0))
    st_n = pl.BlockSpec((rows, hm, dk), lambda i, j: (i, 0, 0))
    st_m = pl.BlockSpec((rows, hm, 1), lambda i, j: (i, 0, 0))
    out_specs = [tok(hm * dv), st_c, st_n, st_m]
    out_shape = [jax.ShapeDtypeStruct((b, s, hm * dv), BF16),
                 jax.ShapeDtypeStruct((b, hm, dv, dk), F32),
                 jax.ShapeDtypeStruct((b, hm, dk), F32),
                 jax.ShapeDtypeStruct((b, hm, 1), F32)]
    if init is not None:
        return pl.pallas_call(
            functools.partial(_mlstm_kernel, L=L, hm=hm, dk=dk, dv=dv),
            grid=(b, s // L),
            in_specs=[tok(hm * dk), tok(hm * dk), tok(hm * dv), tok(hm * dv), tok(LANES),
                      _resident(g_mlstm.shape), st_c, st_n, st_m],
            out_specs=out_specs,
            out_shape=out_shape,
            compiler_params=_params("parallel", "arbitrary"),
            name="mlstm_init",
        )(mq, mk, mv, og, gates, g_mlstm, *init)
    assert L % LANES == 0
    g_t = jnp.broadcast_to(g_mlstm.T[:, :, None], (dv, hm, LANES)).reshape(dv, hm * LANES)
    feat = pl.BlockSpec((rows, hm * dv, L), lambda i, j: (i, 0, j))
    out_specs[0] = feat
    out_shape[0] = jax.ShapeDtypeStruct((b, hm * dv, s), BF16)
    return pl.pallas_call(
        functools.partial(_mlstm_t_kernel, rows=rows, L=L, hm=hm, dk=dk, dv=dv),
        grid=(b // rows, s // L),
        in_specs=[tok(hm * dk), tok(hm * dk), feat, feat, tok(LANES), _resident(g_t.shape)],
        out_specs=out_specs,
        out_shape=out_shape,
        scratch_shapes=[pltpu.VMEM((rows, hm, dv + BF16_ROWS, dk), F32)],
        compiler_params=_params("parallel", "arbitrary"),
        name="mlstm",
    )(mq, mk, mv, og, gates, g_t)


def _diff_lambda(lam_ref, layer):
    lq = lam_ref[...]
    a = jnp.sum(lq[0:1, :] * lq[1:2, :], axis=-1, keepdims=True)
    b = jnp.sum(lq[2:3, :] * lq[3:4, :], axis=-1, keepdims=True)
    lam_init = 0.8 - 0.6 * float(np.exp(-0.3 * layer))
    return jnp.exp(a) - jnp.exp(b) + lam_init, lam_init


def _split_maps(q):
    lane = lax.broadcasted_iota(jnp.int32, q.shape, 1)
    zero = jnp.zeros_like(q)
    half = q.shape[-1] // 2
    return jnp.where(lane < half, q, zero), jnp.where(lane >= half, q, zero)


def _attn_prompt_kernel(q_ref, k_ref, v_ref, lam_ref, g_ref, o_ref, vt_ref, s_ref, p_ref, *, tq, layer):
    s_len, dv = k_ref.shape[1], k_ref.shape[2]
    n_blk = s_len // tq
    for j in range(n_blk):
        vt_ref[:dv, j * tq:(j + 1) * tq] = v_ref[0, j * tq:(j + 1) * tq, :].astype(F32).T.astype(BF16)
    vt_ref[dv:, :] = (lax.broadcasted_iota(jnp.int32, (BF16_ROWS, s_len), 0) == 0).astype(BF16)

    lam, lam_init = _diff_lambda(lam_ref, layer)
    key_chunk = lax.broadcasted_iota(jnp.int32, (tq, tq), 0) // CHUNK
    qry_chunk = lax.broadcasted_iota(jnp.int32, (tq, tq), 1) // CHUNK
    visible = key_chunk <= qry_chunk

    def score_block(slot, q_map, qi, j, m8):
        s = _dot_nt(k_ref[0, j * tq:(j + 1) * tq, :], q_map)
        if j == qi:
            s = jnp.where(visible, s, -jnp.inf)
        s_ref[slot, j] = s
        m_j = jnp.max(s.reshape(tq // 8, 8, tq), axis=0)
        return m_j if m8 is None else jnp.maximum(m8, m_j)

    def weight_block(slot, j, m):
        p_ref[slot, j * tq:(j + 1) * tq, :] = jnp.exp2(s_ref[slot, j] - m).astype(BF16)

    stages = [(qi, c) for qi in range(n_blk) for c in range(2)]
    prev = None
    outs = {}
    for k, stage in enumerate(stages + [None]):
        n_score = 0
        if stage is not None:
            qi, c = stage
            if c == 0:
                q_maps = _split_maps(q_ref[0, qi * tq:(qi + 1) * tq, :])
            n_score = qi + 1
        n_weight = prev[1] + 1 if prev is not None else 0
        m8 = None
        for j in range(max(n_score, n_weight)):
            if j < n_score:
                m8 = score_block(k % 2, q_maps[c], qi, j, m8)
            if j < n_weight:
                weight_block(prev[0] % 2, j, prev[3])
        if prev is not None:
            pk, pqi, pc, _ = prev
            n_keys = (pqi + 1) * tq
            acc = _dot(vt_ref[:, :n_keys], p_ref[pk % 2, :n_keys, :])
            outs[pc] = acc[:dv, :] / acc[dv:dv + 1, :]
            if pc == 1:
                o = (outs[0] - lam * outs[1]).T
                o_ref[0, pqi * tq:(pqi + 1) * tq, :] = (
                    _rms(o) * g_ref[0] * (1.0 - lam_init)).astype(o_ref.dtype)
        if stage is not None:
            prev = (k, qi, c, jnp.max(m8, axis=0, keepdims=True))


def _attn_prompt(dq, dk, dv, lambda_qk, g_diff, *, tq, layer):
    b, s, _ = dq.shape
    hd, dvh = g_diff.shape
    n_blk = s // tq
    head = pl.BlockSpec((1, s, dvh), lambda i, h: (i, 0, h))
    return pl.pallas_call(
        functools.partial(_attn_prompt_kernel, tq=tq, layer=layer),
        grid=(b, hd),
        in_specs=[head, head, head, _resident(lambda_qk.shape),
                  pl.BlockSpec((1, 1, dvh), lambda i, h: (h, 0, 0))],
        out_specs=head,
        out_shape=jax.ShapeDtypeStruct((b, s, hd * dvh), BF16),
        scratch_shapes=[pltpu.VMEM((dvh + BF16_ROWS, s), BF16),
                        pltpu.VMEM((2, n_blk, tq, tq), F32), pltpu.VMEM((2, s, tq), BF16)],
        compiler_params=_params("parallel", "parallel"),
        name="diff_attn_prompt",
    )(dq, dk, dv, lambda_qk, g_diff.reshape(hd, 1, dvh))


def _attn_sample_kernel(q_ref, kc_ref, vc_ref, kn_ref, vn_ref, lam_ref, g_ref, o_ref, *, hd, layer):
    lam, lam_init = _diff_lambda(lam_ref, layer)
    past = kc_ref.shape[1] // hd
    dvh = g_ref.shape[-1]
    for h in range(hd):
        cols = slice(h * dvh, (h + 1) * dvh)
        q_maps = _split_maps(q_ref[0, :, cols])
        k_c = kc_ref[0, pl.ds(h, past, stride=hd), :].astype(BF16)
        v_c = vc_ref[0, pl.ds(h, past, stride=hd), :].astype(BF16)
        k_n = kn_ref[0, :, cols]
        v_n = vn_ref[0, :, cols]
        outs = []
        for c in range(2):
            s_c = _dot_nt(q_maps[c], k_c)
            s_n = _dot_nt(q_maps[c], k_n)
            m = jnp.maximum(jnp.max(s_c, axis=-1, keepdims=True), jnp.max(s_n, axis=-1, keepdims=True))
            p_c = jnp.exp2(s_c - m)
            p_n = jnp.exp2(s_n - m)
            l = jnp.sum(p_c, axis=-1, keepdims=True) + jnp.sum(p_n, axis=-1, keepdims=True)
            outs.append((_dot(p_c.astype(BF16), v_c) + _dot(p_n.astype(BF16), v_n)) / l)
        o = outs[0] - lam * outs[1]
        o_ref[0, :, cols] = (_rms(o) * g_ref[h:h + 1, :] * (1.0 - lam_init)).astype(o_ref.dtype)


def _attn_sample(dq, dk, dv, cache_k, cache_v, lambda_qk, g_diff, *, layer):
    b, t, width = dq.shape
    hd, dvh = g_diff.shape
    new = pl.BlockSpec((1, t, width), lambda i: (i, 0, 0))
    old = pl.BlockSpec((1, cache_k.shape[1], dvh), lambda i: (i, 0, 0))
    return pl.pallas_call(
        functools.partial(_attn_sample_kernel, hd=hd, layer=layer),
        grid=(b,),
        in_specs=[new, old, old, new, new, _resident(lambda_qk.shape), _resident(g_diff.shape)],
        out_specs=new,
        out_shape=jax.ShapeDtypeStruct((b, t, width), BF16),
        compiler_params=_params("parallel"),
        name="diff_attn_sample",
    )(dq, cache_k, cache_v, dk, dv, lambda_qk, g_diff)


def _interleave_kernel(g_ref, u_ref, o_ref):
    for i in range(g_ref.shape[-1] // LANES):
        o_ref[:, 2 * i * LANES:(2 * i + 1) * LANES] = g_ref[0, :, i * LANES:(i + 1) * LANES].astype(BF16)
        o_ref[:, (2 * i + 1) * LANES:(2 * i + 2) * LANES] = u_ref[0, :, i * LANES:(i + 1) * LANES].astype(BF16)


def _interleave_gate_up(w_in_all, layer):
    _, d, two_f = w_in_all.shape
    f = two_f // 2
    n_groups = f // LANES
    per_step = 2 if n_groups % 2 == 0 else 1
    tw = per_step * LANES
    n_steps = n_groups // per_step
    return pl.pallas_call(
        _interleave_kernel,
        grid=(n_steps,),
        in_specs=[pl.BlockSpec((1, d, tw), lambda j: (layer, 0, j)),
                  pl.BlockSpec((1, d, tw), lambda j: (layer, 0, n_steps + j))],
        out_specs=pl.BlockSpec((d, 2 * tw), lambda j: (0, j)),
        out_shape=jax.ShapeDtypeStruct((d, two_f), BF16),
        compiler_params=_params("parallel"),
        name="interleave_gate_up",
    )(w_in_all, w_in_all)


def _rope_tables(pos, dqk):
    rope_dim = dqk // 4
    half = rope_dim // 2
    inv_freq = ROPE_THETA ** (-jnp.arange(half, dtype=F32) * (2.0 / rope_dim))
    ang = pos.astype(F32)[:, None] * inv_freq[None, :]
    cos, sin = jnp.cos(ang), jnp.sin(ang)
    n = pos.shape[0]
    pad = jnp.zeros((n, dqk - rope_dim), F32)
    zero = jnp.zeros((n, half), F32)
    cos_t = jnp.concatenate([cos, cos, pad + 1.0], axis=1)
    sa_t = jnp.concatenate([-sin, zero, pad], axis=1)
    sb_t = jnp.concatenate([zero, sin, pad], axis=1)
    rep = LANES // dqk
    return tuple(jnp.tile(a, (1, rep)) for a in (cos_t, sa_t, sb_t))


def _layer(x, c_mod, pos, attend, mlstm_init, weights, dims, *, nb, t, L, rows, layer):
    (g_norm, w1_in, w1_out, w_proj, w_vo, bias, g_mlstm, g_diff, lambda_qk, w_o, w2_in, w2_out) = weights
    hm, dk_m, widths = dims
    v_transposed = mlstm_init is None
    x1 = _ffn_block(x, c_mod, g_norm, w1_in, w1_out, nb=nb, t=t, g_idx=(0, 1), m_idx=(0, 1, 2))
    dqk = g_diff.shape[1] // 2
    mq, mk, mv, og, dq, dk, dv, k_state, v_state, gates = _project(
        x1, c_mod, g_norm, w_proj, w_vo.T if v_transposed else w_vo, bias, _rope_tables(pos, dqk),
        nb=nb, t=t, hm=hm, widths=widths, k_scale=dk_m ** -0.5, q_scale=dqk ** -0.5 * LOG2_E,
        v_transposed=v_transposed)
    y_m, c_new, n_new, m_new = _mlstm(mq, mk, mv, og, gates, g_mlstm, L=L, rows=rows, init=mlstm_init)
    y_d = attend(dq, dk, dv)
    y = _ffn_block(x1, c_mod, g_norm, w2_in, w2_out, nb=nb, t=t, g_idx=(4, 5), m_idx=(6, 7, 8),
                   heads=(y_m, y_d), w_o=w_o, ym_transposed=v_transposed)
    return y, (k_state, v_state, c_new, n_new, m_new)


def kernel(x_prompt, x_sample, cache_k, cache_v, state_C, state_n, state_m, c_prompt, c_sample,
           w_ada, b_ada, g_norm, w_ffn1_in, w_ffn1_out, w_in, b_igate, b_fgate, g_mlstm, g_diff,
           lambda_qk, w_out, w_ffn2_in, w_ffn2_out):
    depth = w_ada.shape[0]
    bp, sp, d = x_prompt.shape
    bs, ts, _ = x_sample.shape
    past = cache_k.shape[2]
    hm, mv_dim = g_mlstm.shape[1:]
    hd, dv_dim = g_diff.shape[1:]
    mqk = mv_dim // 2
    dqk = dv_dim // 2
    sizes = [hm * mqk, hm * mqk, hm * mv_dim, hm * mv_dim, hm, hm, hd * 2 * dqk, hd * 2 * dqk, hd * dv_dim]
    offs = np.concatenate([[0], np.cumsum(sizes)])
    widths = tuple(sizes[i] for i in (0, 1, 2, 3, 6, 7, 8))
    t_prompt = min(512, sp)
    l_prompt = min(256, sp)
    rows_prompt = max(r for r in (4, 2, 1) if bp % r == 0)

    y_p, y_s = x_prompt, x_sample
    states_p, states_s = [], []
    for l in range(depth):
        mod = _ada(jnp.concatenate([c_prompt, c_sample], axis=0), w_ada[l], b_ada[l])
        mod = mod.reshape(bp + bs, N_ADA, d)
        cols = [w_in[l][:, offs[i]:offs[i + 1]] for i in range(9)]
        gate_cols = jnp.concatenate([cols[4], cols[5], jnp.zeros((d, LANES - 2 * hm), F32)], axis=1)
        w_proj = jnp.concatenate([cols[i] for i in (0, 1, 6, 7, 8)] + [gate_cols], axis=1).astype(BF16)
        w_vo = jnp.concatenate([cols[2], cols[3]], axis=1).astype(BF16)
        bias = jnp.concatenate([b_igate[l], b_fgate[l], jnp.zeros((LANES - 2 * hm,), F32)]).reshape(1, LANES)
        weights = (g_norm[l], _interleave_gate_up(w_ffn1_in, l), w_ffn1_out[l].astype(BF16), w_proj,
                   w_vo, bias, g_mlstm[l], g_diff[l], lambda_qk[l], w_out[l].astype(BF16),
                   _interleave_gate_up(w_ffn2_in, l), w_ffn2_out[l].astype(BF16))
        dims = (hm, mqk, widths)

        attend_p = functools.partial(_attn_prompt, lambda_qk=lambda_qk[l], g_diff=g_diff[l],
                                     tq=min(256, sp), layer=l)
        y_p, st_p = _layer(y_p, mod[:bp], jnp.arange(sp), attend_p, None, weights, dims,
                           nb=1, t=t_prompt, L=l_prompt, rows=rows_prompt, layer=l)

        ck = cache_k[l].reshape(bs, past * hd, 2 * dqk)
        cv = cache_v[l].reshape(bs, past * hd, dv_dim)
        attend_s = functools.partial(_attn_sample, cache_k=ck, cache_v=cv, lambda_qk=lambda_qk[l],
                                     g_diff=g_diff[l], layer=l)
        init_s = (state_C[l], state_n[l], state_m[l].reshape(bs, hm, 1))
        y_s, st_s = _layer(y_s, mod[bp:], past + jnp.arange(ts), attend_s, init_s, weights, dims,
                           nb=bs, t=ts, L=ts, rows=1, layer=l)
        states_p.append(st_p)
        states_s.append(st_s)

    def stack(states, b, s):
        if depth == 1:
            k, v, c, n, m = [a[None] for a in states[0]]
        else:
            k, v, c, n, m = [jnp.stack(a, axis=0) for a in zip(*states)]
        return (k.reshape(depth, b, s, hd, 2 * dqk), v.reshape(depth, b, s, hd, dv_dim),
                c, n, m.reshape(depth, b, hm))

    return (y_p, y_s) + stack(states_p, bp, sp) + stack(states_s, bs, ts)
```

```python
import functools

import jax
import jax.numpy as jnp
import numpy as np
from jax import lax
from jax.experimental import pallas as pl
from jax.experimental.pallas import tpu as pltpu

F32 = jnp.float32
BF16 = jnp.bfloat16

LANES = 128
BF16_ROWS = 16
MXU_N = 256
VMEM_LIMIT_BYTES = 56 << 20
NORM_EPS = 1e-6
ROPE_THETA = 500000.0
N_ADA = 9
CHUNK = 64
LOG2_E = 1.4426950408889634

_NT = (((1,), (1,)), ((), ()))
_TN = (((0,), (0,)), ((), ()))


def _dot(a, b):
    return jnp.dot(a, b, preferred_element_type=F32)


def _dot_nt(a, b):
    return lax.dot_general(a, b, _NT, preferred_element_type=F32)


def _dot_tn(a, b):
    return lax.dot_general(a, b, _TN, preferred_element_type=F32)


def _rms(x):
    return x * lax.rsqrt(jnp.mean(x * x, axis=-1, keepdims=True) + NORM_EPS)


def _silu(x):
    return x * jax.nn.sigmoid(x)


def _params(*sem):
    return pltpu.CompilerParams(dimension_semantics=sem, vmem_limit_bytes=VMEM_LIMIT_BYTES)


def _resident(shape):
    nd = len(shape)
    return pl.BlockSpec(shape, lambda *_: (0,) * nd, pipeline_mode=pl.Buffered(1))


def _ada_kernel(c_ref, w_ref, b_ref, o_ref):
    a = _silu(c_ref[...]).astype(BF16)
    o_ref[...] = _dot(a, w_ref[...].astype(BF16)) + b_ref[...]


def _ada(c, w_ada, b_ada):
    n, d = c.shape
    width = w_ada.shape[1]
    tn = width // 8
    return pl.pallas_call(
        _ada_kernel,
        grid=(width // tn,),
        in_specs=[pl.BlockSpec((n, d), lambda j: (0, 0)),
                  pl.BlockSpec((d, tn), lambda j: (0, j)),
                  pl.BlockSpec((1, tn), lambda j: (0, j))],
        out_specs=pl.BlockSpec((n, tn), lambda j: (0, j)),
        out_shape=jax.ShapeDtypeStruct((n, width), F32),
        compiler_params=_params("parallel"),
        name="ada",
    )(c, w_ada, b_ada.reshape(1, width))


def _ffn_kernel(*refs, nb, t, n_sub, pre_proj, ym_transposed, g_idx, m_idx):
    if pre_proj:
        x_ref, ym_ref, yd_ref, mod_ref, g_ref, wo_ref, w_in_ref, w_out_ref, o_ref, h_ref, act_ref = refs
    else:
        x_ref, mod_ref, g_ref, w_in_ref, w_out_ref, o_ref, h_ref, act_ref = refs
    d = x_ref.shape[-1]
    ts = t // n_sub
    rows = nb * ts
    g_a, g_b = g_idx
    i_shift, i_scale, i_gate = m_idx
    n_groups = w_out_ref.shape[0] // LANES

    def mrow(i):
        return mod_ref[:, i:i + 1, :]

    def grow(i):
        return g_ref[i:i + 1, :][None]

    def prologue(i):
        tok = slice(i * ts, (i + 1) * ts)
        x = x_ref[:, tok, :]
        if pre_proj:
            if ym_transposed:
                w_m = ym_ref.shape[1]
                o = _dot_tn(ym_ref[0, :, tok], wo_ref[:w_m, :]) + _dot(yd_ref[0, tok, :], wo_ref[w_m:, :])
            else:
                heads = jnp.concatenate([ym_ref[:, tok, :], yd_ref[:, tok, :]], axis=-1).reshape(rows, d)
                o = _dot(heads, wo_ref[...])
            x = x + mrow(5) * (_rms(o.reshape(nb, ts, d)) * grow(3))
        h = _rms(x) * grow(g_a) * (1.0 + mrow(i_scale)) + mrow(i_shift)
        h_ref[i * rows:(i + 1) * rows, :] = h.astype(BF16).reshape(rows, d)
        o_ref[:, tok, :] = x

    def chunk(i, j):
        gu = _dot(h_ref[i * rows:(i + 1) * rows, :], w_in_ref[:, j * MXU_N:(j + 1) * MXU_N])
        act = _silu(gu[:, :LANES]) * gu[:, LANES:]
        act_ref[i * rows:(i + 1) * rows, j * LANES:(j + 1) * LANES] = act.astype(BF16)

    def epilogue(i, out):
        tok = slice(i * ts, (i + 1) * ts)
        o_ref[:, tok, :] = o_ref[:, tok, :] + (0.5 * mrow(i_gate)) * (_rms(out.reshape(nb, ts, d)) * grow(g_b))

    prologue(0)
    waiting = None
    for i in range(n_sub):
        for j in range(n_groups):
            chunk(i, j)
            if j == 1:
                if i + 1 < n_sub:
                    prologue(i + 1)
                if waiting is not None:
                    epilogue(*waiting)
                    waiting = None
        waiting = (i, _dot(act_ref[i * rows:(i + 1) * rows, :], w_out_ref[...]))
    epilogue(*waiting)


def _ffn_block(x, mod, g_norm, w_in_r, w_out, *, nb, t, n_sub, g_idx, m_idx, heads=None, w_o=None,
               ym_transposed=False):
    b, s, d = x.shape
    t = t * n_sub
    f = w_out.shape[0]
    pre_proj = heads is not None
    grid = (b // nb, s // t)
    tok = lambda width: pl.BlockSpec((nb, t, width), lambda i, j: (i, j, 0))
    in_specs = [tok(d)]
    args = [x]
    if pre_proj:
        ym, yd = heads
        if ym_transposed:
            assert nb == 1
            ym_spec = pl.BlockSpec((1, ym.shape[1], t), lambda i, j: (i, 0, j))
        else:
            ym_spec = tok(ym.shape[-1])
        in_specs += [ym_spec, tok(yd.shape[-1])]
        args += [ym, yd]
    in_specs += [pl.BlockSpec((nb, N_ADA, d), lambda i, j: (i, 0, 0)), _resident(g_norm.shape)]
    args += [mod, g_norm]
    if pre_proj:
        in_specs.append(_resident(w_o.shape))
        args.append(w_o)
    in_specs += [_resident(w_in_r.shape), _resident(w_out.shape)]
    args += [w_in_r, w_out]
    return pl.pallas_call(
        functools.partial(_ffn_kernel, nb=nb, t=t, n_sub=n_sub, pre_proj=pre_proj, ym_transposed=ym_transposed,
                          g_idx=g_idx, m_idx=m_idx),
        grid=grid,
        in_specs=in_specs,
        out_specs=tok(d),
        out_shape=jax.ShapeDtypeStruct((b, s, d), F32),
        scratch_shapes=[pltpu.VMEM((nb * t, d), BF16), pltpu.VMEM((nb * t, f), BF16)],
        compiler_params=_params("parallel", "parallel"),
        name="ffn_post" if pre_proj else "ffn_pre",
    )(*args)


def _proj_kernel(x_ref, mod_ref, g_ref, w_ref, wv_ref, bias_ref, cos_ref, sa_ref, sb_ref,
                 mq_ref, mk_ref, mv_ref, og_ref, dq_ref, dk_ref, dv_ref, kst_ref, vst_ref, gt_ref, h_ref,
                 *, nb, t, hm, widths, k_scale, q_scale, v_transposed):
    d = x_ref.shape[-1]
    rows = nb * t
    h = _rms(x_ref[...]) * g_ref[2:3, :][None] * (1.0 + mod_ref[:, 4:5, :]) + mod_ref[:, 3:4, :]
    h_ref[...] = h.astype(BF16).reshape(rows, d)

    def mm(c0, width):
        return _dot(h_ref[...], w_ref[:, c0:c0 + width])

    def put(ref, c, val):
        width = val.shape[-1]
        ref[:, :, c:c + width] = val.astype(ref.dtype).reshape(nb, t, width)

    def rope(xb):
        x3 = xb.reshape(nb, t, LANES)
        up = pltpu.roll(xb, LANES - 8, 1).reshape(nb, t, LANES)
        dn = pltpu.roll(xb, 8, 1).reshape(nb, t, LANES)
        r = x3 * cos_ref[...][None] + up * sa_ref[...][None] + dn * sb_ref[...][None]
        return r.reshape(rows, LANES)

    w_mq, w_mk, w_mv, w_mo, w_dq, w_dk, w_dv = widths
    c0 = 0
    for c in range(0, w_mq, MXU_N):
        put(mq_ref, c, mm(c0 + c, min(MXU_N, w_mq - c)))
    c0 += w_mq
    for c in range(0, w_mk, MXU_N):
        put(mk_ref, c, mm(c0 + c, min(MXU_N, w_mk - c)) * k_scale)
    c0 += w_mk
    for c in range(0, w_mv, MXU_N):
        if v_transposed:
            mv_ref[0, c:c + MXU_N, :] = _dot_nt(wv_ref[c:c + MXU_N, :], h_ref[...]).astype(mv_ref.dtype)
        else:
            put(mv_ref, c, _dot(h_ref[...], wv_ref[:, c:c + MXU_N]))
    for c in range(0, w_mo, MXU_N):
        if v_transposed:
            pre = _dot_nt(wv_ref[w_mv + c:w_mv + c + MXU_N, :], h_ref[...])
            og_ref[0, c:c + MXU_N, :] = jax.nn.sigmoid(pre).astype(og_ref.dtype)
        else:
            put(og_ref, c, jax.nn.sigmoid(_dot(h_ref[...], wv_ref[:, w_mv + c:w_mv + c + MXU_N])))
    n_heads = w_dk // LANES

    def put_state(ref, c, val):
        ref[:, pl.ds(c // LANES, t, stride=n_heads), :] = val.reshape(nb, t, LANES)

    for c in range(0, w_dq, MXU_N):
        r = mm(c0 + c, MXU_N)
        for i in range(0, MXU_N, LANES):
            put(dq_ref, c + i, rope(r[:, i:i + LANES]) * q_scale)
    c0 += w_dq
    for c in range(0, w_dk, MXU_N):
        r = mm(c0 + c, MXU_N)
        for i in range(0, MXU_N, LANES):
            k_rot = rope(r[:, i:i + LANES])
            put(dk_ref, c + i, k_rot)
            put_state(kst_ref, c + i, k_rot)
    c0 += w_dk
    for c in range(0, w_dv, MXU_N):
        r = mm(c0 + c, MXU_N)
        put(dv_ref, c, r)
        for i in range(0, MXU_N, LANES):
            put_state(vst_ref, c + i, r[:, i:i + LANES])
    c0 += w_dv
    z = mm(c0, LANES) + bias_ref[...]
    log_sig = jnp.minimum(z, 0.0) - jnp.log1p(jnp.exp(-jnp.abs(z)))
    lane = lax.broadcasted_iota(jnp.int32, z.shape, 1)
    put(gt_ref, 0, jnp.where(lane < hm, z, jnp.where(lane < 2 * hm, log_sig, 0.0)))


def _project(x, mod, g_norm, w_proj, w_mv, bias, tables, *, nb, t, hm, widths, k_scale, q_scale,
             v_transposed):
    b, s, d = x.shape
    cos_t, sa_t, sb_t = tables
    grid = (b // nb, s // t)
    tok = lambda width: pl.BlockSpec((nb, t, width), lambda i, j: (i, j, 0))
    tab = pl.BlockSpec((t, LANES), lambda i, j: (j, 0))
    w_mq, w_mk, w_mv_, w_mo, w_dq, w_dk, w_dv = widths
    n_heads = w_dk // LANES
    out_widths = [(w_mq, BF16), (w_mk, BF16), (w_mv_, BF16), (w_mo, BF16),
                  (w_dq, BF16), (w_dk, BF16), (w_dv, BF16), (0, F32), (0, F32), (LANES, F32)]
    out_specs = [tok(w) for w, _ in out_widths]
    out_shape = [jax.ShapeDtypeStruct((b, s, w), dt) for w, dt in out_widths]
    for i in (7, 8):
        out_specs[i] = pl.BlockSpec((nb, t * n_heads, LANES), lambda i, j: (i, j, 0))
        out_shape[i] = jax.ShapeDtypeStruct((b, s * n_heads, LANES), F32)
    if v_transposed:
        assert nb == 1
        for i, w in ((2, w_mv_), (3, w_mo)):
            out_specs[i] = pl.BlockSpec((1, w, t), lambda i, j: (i, 0, j))
            out_shape[i] = jax.ShapeDtypeStruct((b, w, s), BF16)
    return pl.pallas_call(
        functools.partial(_proj_kernel, nb=nb, t=t, hm=hm, widths=widths,
                          k_scale=k_scale, q_scale=q_scale, v_transposed=v_transposed),
        grid=grid,
        in_specs=[tok(d), pl.BlockSpec((nb, N_ADA, d), lambda i, j: (i, 0, 0)),
                  _resident(g_norm.shape), _resident(w_proj.shape), _resident(w_mv.shape),
                  _resident(bias.shape), tab, tab, tab],
        out_specs=out_specs,
        out_shape=out_shape,
        scratch_shapes=[pltpu.VMEM((nb * t, d), BF16)],
        compiler_params=_params("parallel", "parallel"),
        name="in_proj",
    )(x, mod, g_norm, w_proj, w_mv, bias, cos_t, sa_t, sb_t)


def _mlstm_kernel(q_ref, k_ref, v_ref, og_ref, gt_ref, gm_ref, c0_ref, n0_ref, m0_ref,
                  y_ref, c_ref, n_ref, m_ref, *, L, hm, dk, dv):
    @pl.when(pl.program_id(1) == 0)
    def _():
        c_ref[...] = c0_ref[...]
        n_ref[...] = n0_ref[...]
        m_ref[...] = m0_ref[...]

    gates = gt_ref[0]
    row = lax.broadcasted_iota(jnp.int32, (L, L), 0)
    col = lax.broadcasted_iota(jnp.int32, (L, L), 1)
    causal = col <= row
    diag = col == row

    def as_row(col_vec):
        return jnp.sum(jnp.where(diag, col_vec, 0.0), axis=0, keepdims=True)

    for h in range(hm):
        ig_col = gates[:, h:h + 1]
        lf_row = as_row(gates[:, hm + h:hm + h + 1])
        b_col = jnp.sum(jnp.where(causal, lf_row, 0.0), axis=-1, keepdims=True)
        r_row = as_row(ig_col - b_col)
        m_prev = m_ref[0, h:h + 1, :]
        c_h = c_ref[0, h]
        n_h = n_ref[0, h:h + 1, :]
        q_h = q_ref[0, :, h * dk:(h + 1) * dk]
        k_h = k_ref[0, :, h * dk:(h + 1) * dk]
        v_h = v_ref[0, :, h * dv:(h + 1) * dv]

        dmat = jnp.where(causal, b_col + r_row, -jnp.inf)
        inter = b_col + m_prev
        m_t = jnp.maximum(inter, jnp.max(dmat, axis=-1, keepdims=True))
        w_inter = jnp.exp(inter - m_t)
        s = _dot_nt(q_h, k_h) * jnp.exp(dmat - m_t)
        num = _dot(s.astype(BF16), v_h) + _dot_nt(q_h, c_h.astype(BF16)) * w_inter
        qn = jnp.sum(q_h.astype(F32) * n_h, axis=-1, keepdims=True)
        den = jnp.sum(s, axis=-1, keepdims=True) + w_inter * qn
        den = jnp.maximum(jnp.abs(den), jnp.exp(-m_t))
        hid = num / den
        y = _rms(hid) * gm_ref[h:h + 1, :] * og_ref[0, :, h * dv:(h + 1) * dv].astype(F32)
        y_ref[0, :, h * dv:(h + 1) * dv] = y.astype(y_ref.dtype)

        m_last = m_t[L - 1:L, :]
        b_last = b_col[L - 1:L, :]
        w_state = jnp.exp(b_last + m_prev - m_last)
        w_rows = jnp.exp(b_last - b_col + ig_col - m_last)
        kw = k_h.astype(F32) * w_rows
        c_ref[0, h] = w_state * c_h + _dot_tn(v_h, kw.astype(BF16))
        n_ref[0, h:h + 1, :] = w_state * n_h + jnp.sum(kw, axis=0, keepdims=True)
        m_ref[0, h:h + 1, :] = m_last


def _mlstm_t_kernel(q_ref, k_ref, vt_ref, og_ref, gt_ref, gmt_ref, y_ref, c_ref, n_ref, m_ref, st_ref,
                    *, rows, L, hm, dk, dv):
    ci = pl.program_id(1)

    @pl.when(ci == 0)
    def _():
        st_ref[...] = jnp.zeros_like(st_ref)
        m_ref[...] = jnp.zeros_like(m_ref)

    src = lax.broadcasted_iota(jnp.int32, (L, L), 0)
    tgt = lax.broadcasted_iota(jnp.int32, (L, L), 1)
    visible = src <= tgt
    lower = (tgt <= src).astype(BF16)
    ones_rows = (lax.broadcasted_iota(jnp.int32, (BF16_ROWS, L), 0) == 0).astype(BF16)

    gates_all = jnp.concatenate([gt_ref[r] for r in range(rows)], axis=-1)
    g1 = gates_all.astype(BF16)
    r1 = gates_all - g1.astype(F32)
    g2 = r1.astype(BF16)
    g3 = (r1 - g2.astype(F32)).astype(BF16)
    csum_all = _dot(lower, g1) + _dot(lower, g2) + _dot(lower, g3)

    row_vals = {}

    def decay_stage(r, h):
        if r not in row_vals:
            csum = csum_all[:, r * LANES:(r + 1) * LANES]
            row_vals[r] = (gt_ref[r], csum, csum.T)
        gates, csum, csum_t = row_vals[r]
        r_col = gates[:, h:h + 1] - csum[:, hm + h:hm + h + 1]
        r_lanes = jnp.broadcast_to(r_col, (L, LANES))
        b_row = csum_t[hm + h:hm + h + 1, :]
        m_prev = m_ref[r, h:h + 1, :]
        dmat = jnp.where(visible, jnp.concatenate([r_lanes] * (L // LANES), axis=-1) + b_row, -jnp.inf)
        inter = b_row + m_prev
        m_t = jnp.maximum(inter, jnp.max(dmat, axis=0, keepdims=True))
        m_last = m_t[:, L - 1:L]
        b_last = b_row[:, L - 1:L]
        return dict(
            w_intra=jnp.exp(dmat - m_t), w_inter=jnp.exp(inter - m_t), floor=jnp.exp(-m_t),
            w_state=jnp.exp(b_last + m_prev - m_last), w_rows=jnp.exp(r_lanes[:, :dk] + (b_last - m_last)),
            m_last=m_last)

    def matmul_stage(r, h, w):
        state = st_ref[r, h]
        q_h = q_ref[r, :, h * dk:(h + 1) * dk]
        k_h = k_ref[r, :, h * dk:(h + 1) * dk]
        vt_h = vt_ref[r, h * dv:(h + 1) * dv, :]
        s = _dot_nt(k_h, q_h) * w["w_intra"]
        sq = _dot_nt(state.astype(BF16), q_h)
        num = _dot(vt_h, s.astype(BF16)) + sq[:dv, :] * w["w_inter"]
        den = jnp.sum(s, axis=0, keepdims=True) + w["w_inter"] * sq[dv:dv + 1, :]
        hid = num / jnp.maximum(jnp.abs(den), w["floor"])
        scale = lax.rsqrt(jnp.mean(hid * hid, axis=0, keepdims=True) + NORM_EPS)
        gain = jnp.concatenate([gmt_ref[:, h * LANES:(h + 1) * LANES]] * (L // LANES), axis=-1)
        y_t = hid * scale * gain * og_ref[r, h * dv:(h + 1) * dv, :].astype(F32)
        y_ref[r, h * dv:(h + 1) * dv, :] = y_t.astype(y_ref.dtype)

        kw = (k_h.astype(F32) * w["w_rows"]).astype(BF16)
        vt_aug = jnp.concatenate([vt_h, ones_rows], axis=0)
        st_ref[r, h] = w["w_state"] * state + _dot(vt_aug, kw)
        m_ref[r, h:h + 1, :] = w["m_last"]

    chains = [(r, h) for r in range(rows) for h in range(hm)]
    pending = None
    for chain in chains + [None]:
        weights = decay_stage(*chain) if chain is not None else None
        if pending is not None:
            matmul_stage(*pending)
        pending = chain + (weights,) if chain is not None else None

    @pl.when(ci == pl.num_programs(1) - 1)
    def _():
        for r in range(rows):
            for h in range(hm):
                c_ref[r, h] = st_ref[r, h, :dv, :]
                n_ref[r, h:h + 1, :] = st_ref[r, h, dv:dv + 1, :]


def _mlstm(mq, mk, mv, og, gates, g_mlstm, *, L, rows=1, init=None):
    b, s, _ = mq.shape
    hm, dv = g_mlstm.shape
    dk = mq.shape[-1] // hm
    assert init is None or rows == 1
    tok = lambda width: pl.BlockSpec((rows, L, width), lambda i, j: (i, j, 0))
    st_c = pl.BlockSpec((rows, hm, dv, dk), lambda i, j: (i, 0, 0, 0))
    st_n = pl.BlockSpec((rows, hm, dk), lambda i, j: (i, 0, 0))
    st_m = pl.BlockSpec((rows, hm, 1), lambda i, j: (i, 0, 0))
    out_specs = [tok(hm * dv), st_c, st_n, st_m]
    out_shape = [jax.ShapeDtypeStruct((b, s, hm * dv), BF16),
                 jax.ShapeDtypeStruct((b, hm, dv, dk), F32),
                 jax.ShapeDtypeStruct((b, hm, dk), F32),
                 jax.ShapeDtypeStruct((b, hm, 1), F32)]
    if init is not None:
        return pl.pallas_call(
            functools.partial(_mlstm_kernel, L=L, hm=hm, dk=dk, dv=dv),
            grid=(b, s // L),
            in_specs=[tok(hm * dk), tok(hm * dk), tok(hm * dv), tok(hm * dv), tok(LANES),
                      _resident(g_mlstm.shape), st_c, st_n, st_m],
            out_specs=out_specs,
            out_shape=out_shape,
            compiler_params=_params("parallel", "arbitrary"),
            name="mlstm_init",
        )(mq, mk, mv, og, gates, g_mlstm, *init)
    assert L % LANES == 0
    g_t = jnp.broadcast_to(g_mlstm.T[:, :, None], (dv, hm, LANES)).reshape(dv, hm * LANES)
    feat = pl.BlockSpec((rows, hm * dv, L), lambda i, j: (i, 0, j))
    out_specs[0] = feat
    out_shape[0] = jax.ShapeDtypeStruct((b, hm * dv, s), BF16)
    return pl.pallas_call(
        functools.partial(_mlstm_t_kernel, rows=rows, L=L, hm=hm, dk=dk, dv=dv),
        grid=(b // rows, s // L),
        in_specs=[tok(hm * dk), tok(hm * dk), feat, feat, tok(LANES), _resident(g_t.shape)],
        out_specs=out_specs,
        out_shape=out_shape,
        scratch_shapes=[pltpu.VMEM((rows, hm, dv + BF16_ROWS, dk), F32)],
        compiler_params=_params("parallel", "arbitrary"),
        name="mlstm",
    )(mq, mk, mv, og, gates, g_t)


def _diff_lambda(lam_ref, layer):
    lq = lam_ref[...]
    a = jnp.sum(lq[0:1, :] * lq[1:2, :], axis=-1, keepdims=True)
    b = jnp.sum(lq[2:3, :] * lq[3:4, :], axis=-1, keepdims=True)
    lam_init = 0.8 - 0.6 * float(np.exp(-0.3 * layer))
    return jnp.exp(a) - jnp.exp(b) + lam_init, lam_init


def _split_maps(q):
    lane = lax.broadcasted_iota(jnp.int32, q.shape, 1)
    zero = jnp.zeros_like(q)
    half = q.shape[-1] // 2
    return jnp.where(lane < half, q, zero), jnp.where(lane >= half, q, zero)


def _attn_prompt_kernel(q_ref, k_ref, v_ref, lam_ref, g_ref, o_ref, vt_ref, s_ref, p_ref, *, tq, layer):
    s_len = k_ref.shape[1]
    n_heads, dv = g_ref.shape[0], g_ref.shape[-1]
    n_blk = s_len // tq
    for hh in range(n_heads):
        vt_ref[hh, dv:, :] = (lax.broadcasted_iota(jnp.int32, (BF16_ROWS, s_len), 0) == 0).astype(BF16)

    def lanes(hh):
        return slice(hh * dv, (hh + 1) * dv)

    def transpose_values(hh, j):
        v_blk = v_ref[0, j * tq:(j + 1) * tq, lanes(hh)]
        vt_ref[hh, :dv, j * tq:(j + 1) * tq] = v_blk.astype(F32).T.astype(BF16)

    lam, lam_init = _diff_lambda(lam_ref, layer)
    key_chunk = lax.broadcasted_iota(jnp.int32, (tq, tq), 0) // CHUNK
    qry_chunk = lax.broadcasted_iota(jnp.int32, (tq, tq), 1) // CHUNK
    visible = key_chunk <= qry_chunk

    def score_block(slot, hh, q_map, qi, j, m8):
        s = _dot_nt(k_ref[0, j * tq:(j + 1) * tq, lanes(hh)], q_map)
        if j == qi:
            s = jnp.where(visible, s, -jnp.inf)
        s_ref[slot, j] = s
        m_j = jnp.max(s.reshape(tq // 8, 8, tq), axis=0)
        return m_j if m8 is None else jnp.maximum(m8, m_j)

    def weight_block(slot, j, m):
        p_ref[slot, j * tq:(j + 1) * tq, :] = jnp.exp2(s_ref[slot, j] - m).astype(BF16)

    stages = [(hh, qi, c) for hh in range(n_heads) for qi in reversed(range(n_blk)) for c in range(2)]
    prev = None
    outs = {}
    for k, stage in enumerate(stages + [None]):
        n_score = 0
        if stage is not None:
            hh, qi, c = stage
            if c == 0:
                q_maps = _split_maps(q_ref[0, qi * tq:(qi + 1) * tq, lanes(hh)])
            n_score = qi + 1
        n_weight = prev[2] + 1 if prev is not None else 0
        m8 = None
        for j in range(max(n_score, n_weight)):
            if j < n_score:
                m8 = score_block(k % 2, hh, q_maps[c], qi, j, m8)
                if qi == n_blk - 1 and c == 0:
                    transpose_values(hh, j)
            if j < n_weight:
                weight_block(prev[0] % 2, j, prev[4])
        if prev is not None:
            pk, ph, pqi, pc, _ = prev
            n_keys = (pqi + 1) * tq
            acc = _dot(vt_ref[ph, :, :n_keys], p_ref[pk % 2, :n_keys, :])
            outs[pc] = acc[:dv, :] / acc[dv:dv + 1, :]
            if pc == 1:
                o = (outs[0] - lam * outs[1]).T
                o_ref[0, pqi * tq:(pqi + 1) * tq, lanes(ph)] = (
                    _rms(o) * g_ref[ph] * (1.0 - lam_init)).astype(o_ref.dtype)
        if stage is not None:
            prev = (k, hh, qi, c, jnp.max(m8, axis=0, keepdims=True))


def _attn_prompt(dq, dk, dv, lambda_qk, g_diff, *, tq, heads, layer):
    b, s, _ = dq.shape
    hd, dvh = g_diff.shape
    n_blk = s // tq
    head = pl.BlockSpec((1, s, heads * dvh), lambda i, h: (i, 0, h))
    return pl.pallas_call(
        functools.partial(_attn_prompt_kernel, tq=tq, layer=layer),
        grid=(b, hd // heads),
        in_specs=[head, head, head, _resident(lambda_qk.shape),
                  pl.BlockSpec((heads, 1, dvh), lambda i, h: (h, 0, 0))],
        out_specs=head,
        out_shape=jax.ShapeDtypeStruct((b, s, hd * dvh), BF16),
        scratch_shapes=[pltpu.VMEM((heads, dvh + BF16_ROWS, s), BF16),
                        pltpu.VMEM((2, n_blk, tq, tq), F32), pltpu.VMEM((2, s, tq), BF16)],
        compiler_params=_params("parallel", "parallel"),
        name="diff_attn_prompt",
    )(dq, dk, dv, lambda_qk, g_diff.reshape(hd, 1, dvh))


def _attn_sample_kernel(q_ref, kc_ref, vc_ref, kn_ref, vn_ref, lam_ref, g_ref, o_ref, *, hd, layer):
    lam, lam_init = _diff_lambda(lam_ref, layer)
    past = kc_ref.shape[1] // hd
    dvh = g_ref.shape[-1]
    for h in range(hd):
        cols = slice(h * dvh, (h + 1) * dvh)
        q_maps = _split_maps(q_ref[0, :, cols])
        k_c = kc_ref[0, pl.ds(h, past, stride=hd), :].astype(BF16)
        v_c = vc_ref[0, pl.ds(h, past, stride=hd), :].astype(BF16)
        k_n = kn_ref[0, :, cols]
        v_n = vn_ref[0, :, cols]
        outs = []
        for c in range(2):
            s_c = _dot_nt(q_maps[c], k_c)
            s_n = _dot_nt(q_maps[c], k_n)
            m = jnp.maximum(jnp.max(s_c, axis=-1, keepdims=True), jnp.max(s_n, axis=-1, keepdims=True))
            p_c = jnp.exp2(s_c - m)
            p_n = jnp.exp2(s_n - m)
            l = jnp.sum(p_c, axis=-1, keepdims=True) + jnp.sum(p_n, axis=-1, keepdims=True)
            outs.append((_dot(p_c.astype(BF16), v_c) + _dot(p_n.astype(BF16), v_n)) / l)
        o = outs[0] - lam * outs[1]
        o_ref[0, :, cols] = (_rms(o) * g_ref[h:h + 1, :] * (1.0 - lam_init)).astype(o_ref.dtype)


def _attn_sample(dq, dk, dv, cache_k, cache_v, lambda_qk, g_diff, *, layer):
    b, t, width = dq.shape
    hd, dvh = g_diff.shape
    new = pl.BlockSpec((1, t, width), lambda i: (i, 0, 0))
    old = pl.BlockSpec((1, cache_k.shape[1], dvh), lambda i: (i, 0, 0))
    return pl.pallas_call(
        functools.partial(_attn_sample_kernel, hd=hd, layer=layer),
        grid=(b,),
        in_specs=[new, old, old, new, new, _resident(lambda_qk.shape), _resident(g_diff.shape)],
        out_specs=new,
        out_shape=jax.ShapeDtypeStruct((b, t, width), BF16),
        compiler_params=_params("parallel"),
        name="diff_attn_sample",
    )(dq, cache_k, cache_v, dk, dv, lambda_qk, g_diff)


def _interleave_kernel(g_ref, u_ref, o_ref):
    for i in range(g_ref.shape[-1] // LANES):
        o_ref[:, 2 * i * LANES:(2 * i + 1) * LANES] = g_ref[0, :, i * LANES:(i + 1) * LANES].astype(BF16)
        o_ref[:, (2 * i + 1) * LANES:(2 * i + 2) * LANES] = u_ref[0, :, i * LANES:(i + 1) * LANES].astype(BF16)


def _interleave_gate_up(w_in_all, layer):
    _, d, two_f = w_in_all.shape
    f = two_f // 2
    n_groups = f // LANES
    per_step = 2 if n_groups % 2 == 0 else 1
    tw = per_step * LANES
    n_steps = n_groups // per_step
    return pl.pallas_call(
        _interleave_kernel,
        grid=(n_steps,),
        in_specs=[pl.BlockSpec((1, d, tw), lambda j: (layer, 0, j)),
                  pl.BlockSpec((1, d, tw), lambda j: (layer, 0, n_steps + j))],
        out_specs=pl.BlockSpec((d, 2 * tw), lambda j: (0, j)),
        out_shape=jax.ShapeDtypeStruct((d, two_f), BF16),
        compiler_params=_params("parallel"),
        name="interleave_gate_up",
    )(w_in_all, w_in_all)


def _rope_tables(pos, dqk):
    rope_dim = dqk // 4
    half = rope_dim // 2
    inv_freq = ROPE_THETA ** (-jnp.arange(half, dtype=F32) * (2.0 / rope_dim))
    ang = pos.astype(F32)[:, None] * inv_freq[None, :]
    cos, sin = jnp.cos(ang), jnp.sin(ang)
    n = pos.shape[0]
    pad = jnp.zeros((n, dqk - rope_dim), F32)
    zero = jnp.zeros((n, half), F32)
    cos_t = jnp.concatenate([cos, cos, pad + 1.0], axis=1)
    sa_t = jnp.concatenate([-sin, zero, pad], axis=1)
    sb_t = jnp.concatenate([zero, sin, pad], axis=1)
    rep = LANES // dqk
    return tuple(jnp.tile(a, (1, rep)) for a in (cos_t, sa_t, sb_t))


def _layer(x, c_mod, pos, attend, mlstm_init, weights, dims, *, nb, t, n_sub, L, rows, layer):
    (g_norm, w1_in, w1_out, w_proj, w_vo, bias, g_mlstm, g_diff, lambda_qk, w_o, w2_in, w2_out) = weights
    hm, dk_m, widths = dims
    v_transposed = mlstm_init is None
    x1 = _ffn_block(x, c_mod, g_norm, w1_in, w1_out, nb=nb, t=t, n_sub=n_sub, g_idx=(0, 1), m_idx=(0, 1, 2))
    dqk = g_diff.shape[1] // 2
    mq, mk, mv, og, dq, dk, dv, k_state, v_state, gates = _project(
        x1, c_mod, g_norm, w_proj, w_vo.T if v_transposed else w_vo, bias, _rope_tables(pos, dqk),
        nb=nb, t=t, hm=hm, widths=widths, k_scale=dk_m ** -0.5, q_scale=dqk ** -0.5 * LOG2_E,
        v_transposed=v_transposed)
    y_m, c_new, n_new, m_new = _mlstm(mq, mk, mv, og, gates, g_mlstm, L=L, rows=rows, init=mlstm_init)
    y_d = attend(dq, dk, dv)
    y = _ffn_block(x1, c_mod, g_norm, w2_in, w2_out, nb=nb, t=t, n_sub=n_sub, g_idx=(4, 5), m_idx=(6, 7, 8),
                   heads=(y_m, y_d), w_o=w_o, ym_transposed=v_transposed)
    return y, (k_state, v_state, c_new, n_new, m_new)


def kernel(x_prompt, x_sample, cache_k, cache_v, state_C, state_n, state_m, c_prompt, c_sample,
           w_ada, b_ada, g_norm, w_ffn1_in, w_ffn1_out, w_in, b_igate, b_fgate, g_mlstm, g_diff,
           lambda_qk, w_out, w_ffn2_in, w_ffn2_out):
    depth = w_ada.shape[0]
    bp, sp, d = x_prompt.shape
    bs, ts, _ = x_sample.shape
    past = cache_k.shape[2]
    hm, mv_dim = g_mlstm.shape[1:]
    hd, dv_dim = g_diff.shape[1:]
    mqk = mv_dim // 2
    dqk = dv_dim // 2
    sizes = [hm * mqk, hm * mqk, hm * mv_dim, hm * mv_dim, hm, hm, hd * 2 * dqk, hd * 2 * dqk, hd * dv_dim]
    offs = np.concatenate([[0], np.cumsum(sizes)])
    widths = tuple(sizes[i] for i in (0, 1, 2, 3, 6, 7, 8))
    t_prompt = min(512, sp)
    sub_prompt = 2 if sp % (2 * t_prompt) == 0 else 1
    l_prompt = min(256, sp)
    rows_prompt = max(r for r in (4, 2, 1) if bp % r == 0)

    y_p, y_s = x_prompt, x_sample
    states_p, states_s = [], []
    for l in range(depth):
        mod = _ada(jnp.concatenate([c_prompt, c_sample], axis=0), w_ada[l], b_ada[l])
        mod = mod.reshape(bp + bs, N_ADA, d)
        cols = [w_in[l][:, offs[i]:offs[i + 1]] for i in range(9)]
        gate_cols = jnp.concatenate([cols[4], cols[5], jnp.zeros((d, LANES - 2 * hm), F32)], axis=1)
        w_proj = jnp.concatenate([cols[i] for i in (0, 1, 6, 7, 8)] + [gate_cols], axis=1).astype(BF16)
        w_vo = jnp.concatenate([cols[2], cols[3]], axis=1).astype(BF16)
        bias = jnp.concatenate([b_igate[l], b_fgate[l], jnp.zeros((LANES - 2 * hm,), F32)]).reshape(1, LANES)
        weights = (g_norm[l], _interleave_gate_up(w_ffn1_in, l), w_ffn1_out[l].astype(BF16), w_proj,
                   w_vo, bias, g_mlstm[l], g_diff[l], lambda_qk[l], w_out[l].astype(BF16),
                   _interleave_gate_up(w_ffn2_in, l), w_ffn2_out[l].astype(BF16))
        dims = (hm, mqk, widths)

        attend_p = functools.partial(_attn_prompt, lambda_qk=lambda_qk[l], g_diff=g_diff[l],
                                     tq=min(256, sp), heads=2 if hd % 2 == 0 else 1, layer=l)
        y_p, st_p = _layer(y_p, mod[:bp], jnp.arange(sp), attend_p, None, weights, dims,
                           nb=1, t=t_prompt, n_sub=sub_prompt, L=l_prompt, rows=rows_prompt, layer=l)

        ck = cache_k[l].reshape(bs, past * hd, 2 * dqk)
        cv = cache_v[l].reshape(bs, past * hd, dv_dim)
        attend_s = functools.partial(_attn_sample, cache_k=ck, cache_v=cv, lambda_qk=lambda_qk[l],
                                     g_diff=g_diff[l], layer=l)
        init_s = (state_C[l], state_n[l], state_m[l].reshape(bs, hm, 1))
        y_s, st_s = _layer(y_s, mod[bp:], past + jnp.arange(ts), attend_s, init_s, weights, dims,
                           nb=bs, t=ts, n_sub=1, L=ts, rows=1, layer=l)
        states_p.append(st_p)
        states_s.append(st_s)

    def stack(states, b, s):
        if depth == 1:
            k, v, c, n, m = [a[None] for a in states[0]]
        else:
            k, v, c, n, m = [jnp.stack(a, axis=0) for a in zip(*states)]
        return (k.reshape(depth, b, s, hd, 2 * dqk), v.reshape(depth, b, s, hd, dv_dim),
                c, n, m.reshape(depth, b, hm))

    return (y_p, y_s) + stack(states_p, bp, sp) + stack(states_s, bs, ts)
```

```python
import functools

import jax
import jax.numpy as jnp
import numpy as np
from jax import lax
from jax.experimental import pallas as pl
from jax.experimental.pallas import tpu as pltpu

F32 = jnp.float32
BF16 = jnp.bfloat16

LANES = 128
BF16_ROWS = 16
MXU_N = 256
VMEM_LIMIT_BYTES = 56 << 20
NORM_EPS = 1e-6
ROPE_THETA = 500000.0
N_ADA = 9
CHUNK = 64
LOG2_E = 1.4426950408889634

_NT = (((1,), (1,)), ((), ()))
_TN = (((0,), (0,)), ((), ()))


def _dot(a, b):
    return jnp.dot(a, b, preferred_element_type=F32)


def _dot_nt(a, b):
    return lax.dot_general(a, b, _NT, preferred_element_type=F32)


def _dot_tn(a, b):
    return lax.dot_general(a, b, _TN, preferred_element_type=F32)


def _rms(x):
    return x * lax.rsqrt(jnp.mean(x * x, axis=-1, keepdims=True) + NORM_EPS)


def _silu(x):
    return x * jax.nn.sigmoid(x)


def _params(*sem):
    return pltpu.CompilerParams(dimension_semantics=sem, vmem_limit_bytes=VMEM_LIMIT_BYTES)


def _resident(shape):
    nd = len(shape)
    return pl.BlockSpec(shape, lambda *_: (0,) * nd, pipeline_mode=pl.Buffered(1))


def _ada_kernel(c_ref, w_ref, b_ref, o_ref):
    a = _silu(c_ref[...]).astype(BF16)
    o_ref[...] = _dot(a, w_ref[...].astype(BF16)) + b_ref[...]


def _ada(c, w_ada, b_ada):
    n, d = c.shape
    width = w_ada.shape[1]
    tn = width // 8
    return pl.pallas_call(
        _ada_kernel,
        grid=(width // tn,),
        in_specs=[pl.BlockSpec((n, d), lambda j: (0, 0)),
                  pl.BlockSpec((d, tn), lambda j: (0, j)),
                  pl.BlockSpec((1, tn), lambda j: (0, j))],
        out_specs=pl.BlockSpec((n, tn), lambda j: (0, j)),
        out_shape=jax.ShapeDtypeStruct((n, width), F32),
        compiler_params=_params("parallel"),
        name="ada",
    )(c, w_ada, b_ada.reshape(1, width))


def _ffn_kernel(*refs, nb, t, n_sub, pre_proj, ym_transposed, g_idx, m_idx):
    if pre_proj:
        x_ref, ym_ref, yd_ref, mod_ref, g_ref, wo_ref, w_in_ref, w_out_ref, o_ref, h_ref, act_ref = refs
    else:
        x_ref, mod_ref, g_ref, w_in_ref, w_out_ref, o_ref, h_ref, act_ref = refs
    d = x_ref.shape[-1]
    ts = t // n_sub
    rows = nb * ts
    g_a, g_b = g_idx
    i_shift, i_scale, i_gate = m_idx
    n_groups = w_out_ref.shape[0] // LANES

    def mrow(i):
        return mod_ref[:, i:i + 1, :]

    def grow(i):
        return g_ref[i:i + 1, :][None]

    def prologue(i):
        tok = slice(i * ts, (i + 1) * ts)
        x = x_ref[:, tok, :]
        if pre_proj:
            if ym_transposed:
                w_m = ym_ref.shape[1]
                o = _dot_tn(ym_ref[0, :, tok], wo_ref[:w_m, :]) + _dot(yd_ref[0, tok, :], wo_ref[w_m:, :])
            else:
                heads = jnp.concatenate([ym_ref[:, tok, :], yd_ref[:, tok, :]], axis=-1).reshape(rows, d)
                o = _dot(heads, wo_ref[...])
            x = x + mrow(5) * (_rms(o.reshape(nb, ts, d)) * grow(3))
        h = _rms(x) * grow(g_a) * (1.0 + mrow(i_scale)) + mrow(i_shift)
        h_ref[i * rows:(i + 1) * rows, :] = h.astype(BF16).reshape(rows, d)
        o_ref[:, tok, :] = x

    def chunk(i, j):
        gu = _dot(h_ref[i * rows:(i + 1) * rows, :], w_in_ref[:, j * MXU_N:(j + 1) * MXU_N])
        act = _silu(gu[:, :LANES]) * gu[:, LANES:]
        act_ref[i * rows:(i + 1) * rows, j * LANES:(j + 1) * LANES] = act.astype(BF16)

    def epilogue(i, out):
        tok = slice(i * ts, (i + 1) * ts)
        o_ref[:, tok, :] = o_ref[:, tok, :] + (0.5 * mrow(i_gate)) * (_rms(out.reshape(nb, ts, d)) * grow(g_b))

    prologue(0)
    waiting = None
    for i in range(n_sub):
        for j in range(n_groups):
            chunk(i, j)
            if j == 1:
                if i + 1 < n_sub:
                    prologue(i + 1)
                if waiting is not None:
                    epilogue(*waiting)
                    waiting = None
        waiting = (i, _dot(act_ref[i * rows:(i + 1) * rows, :], w_out_ref[...]))
    epilogue(*waiting)


def _ffn_block(x, mod, g_norm, w_in_r, w_out, *, nb, t, n_sub, g_idx, m_idx, heads=None, w_o=None,
               ym_transposed=False):
    b, s, d = x.shape
    t = t * n_sub
    f = w_out.shape[0]
    pre_proj = heads is not None
    grid = (b // nb, s // t)
    tok = lambda width: pl.BlockSpec((nb, t, width), lambda i, j: (i, j, 0))
    in_specs = [tok(d)]
    args = [x]
    if pre_proj:
        ym, yd = heads
        if ym_transposed:
            assert nb == 1
            ym_spec = pl.BlockSpec((1, ym.shape[1], t), lambda i, j: (i, 0, j))
        else:
            ym_spec = tok(ym.shape[-1])
        in_specs += [ym_spec, tok(yd.shape[-1])]
        args += [ym, yd]
    in_specs += [pl.BlockSpec((nb, N_ADA, d), lambda i, j: (i, 0, 0)), _resident(g_norm.shape)]
    args += [mod, g_norm]
    if pre_proj:
        in_specs.append(_resident(w_o.shape))
        args.append(w_o)
    in_specs += [_resident(w_in_r.shape), _resident(w_out.shape)]
    args += [w_in_r, w_out]
    return pl.pallas_call(
        functools.partial(_ffn_kernel, nb=nb, t=t, n_sub=n_sub, pre_proj=pre_proj, ym_transposed=ym_transposed,
                          g_idx=g_idx, m_idx=m_idx),
        grid=grid,
        in_specs=in_specs,
        out_specs=tok(d),
        out_shape=jax.ShapeDtypeStruct((b, s, d), F32),
        scratch_shapes=[pltpu.VMEM((nb * t, d), BF16), pltpu.VMEM((nb * t, f), BF16)],
        compiler_params=_params("parallel", "parallel"),
        name="ffn_post" if pre_proj else "ffn_pre",
    )(*args)


def _proj_kernel(x_ref, mod_ref, g_ref, w_ref, wv_ref, bias_ref, cos_ref, sa_ref, sb_ref,
                 mq_ref, mk_ref, mv_ref, og_ref, dq_ref, dk_ref, dv_ref, kst_ref, vst_ref, gt_ref, h_ref,
                 *, nb, t, n_sub, hm, widths, k_scale, q_scale, v_transposed):
    d = x_ref.shape[-1]
    ts = t // n_sub
    rows = nb * ts
    w_mq, w_mk, w_mv, w_mo, w_dq, w_dk, w_dv = widths
    n_heads = w_dk // LANES

    def prologue(i):
        tok = slice(i * ts, (i + 1) * ts)
        h = _rms(x_ref[:, tok, :]) * g_ref[2:3, :][None] * (1.0 + mod_ref[:, 4:5, :]) + mod_ref[:, 3:4, :]
        h_ref[i * rows:(i + 1) * rows, :] = h.astype(BF16).reshape(rows, d)

    def project(i):
        tok = slice(i * ts, (i + 1) * ts)
        h_rows = slice(i * rows, (i + 1) * rows)

        def mm(c0, width):
            return _dot(h_ref[h_rows, :], w_ref[:, c0:c0 + width])

        def put(ref, c, val):
            width = val.shape[-1]
            ref[:, tok, c:c + width] = val.astype(ref.dtype).reshape(nb, ts, width)

        def put_state(ref, c, val):
            ref[:, pl.ds(i * ts * n_heads + c // LANES, ts, stride=n_heads), :] = val.reshape(nb, ts, LANES)

        def rope(xb):
            x3 = xb.reshape(nb, ts, LANES)
            up = pltpu.roll(xb, LANES - 8, 1).reshape(nb, ts, LANES)
            dn = pltpu.roll(xb, 8, 1).reshape(nb, ts, LANES)
            r = x3 * cos_ref[tok, :][None] + up * sa_ref[tok, :][None] + dn * sb_ref[tok, :][None]
            return r.reshape(rows, LANES)

        c0 = 0
        for c in range(0, w_mq, MXU_N):
            put(mq_ref, c, mm(c0 + c, min(MXU_N, w_mq - c)))
        c0 += w_mq
        if i + 1 < n_sub:
            prologue(i + 1)
        for c in range(0, w_mk, MXU_N):
            put(mk_ref, c, mm(c0 + c, min(MXU_N, w_mk - c)) * k_scale)
        c0 += w_mk
        for c in range(0, w_mv, MXU_N):
            if v_transposed:
                mv_ref[0, c:c + MXU_N, tok] = _dot_nt(wv_ref[c:c + MXU_N, :], h_ref[h_rows, :]).astype(mv_ref.dtype)
            else:
                put(mv_ref, c, _dot(h_ref[h_rows, :], wv_ref[:, c:c + MXU_N]))
        for c in range(0, w_mo, MXU_N):
            if v_transposed:
                pre = _dot_nt(wv_ref[w_mv + c:w_mv + c + MXU_N, :], h_ref[h_rows, :])
                og_ref[0, c:c + MXU_N, tok] = jax.nn.sigmoid(pre).astype(og_ref.dtype)
            else:
                put(og_ref, c, jax.nn.sigmoid(_dot(h_ref[h_rows, :], wv_ref[:, w_mv + c:w_mv + c + MXU_N])))
        for c in range(0, w_dq, MXU_N):
            r = mm(c0 + c, MXU_N)
            for k in range(0, MXU_N, LANES):
                put(dq_ref, c + k, rope(r[:, k:k + LANES]) * q_scale)
        c0 += w_dq
        for c in range(0, w_dk, MXU_N):
            r = mm(c0 + c, MXU_N)
            for k in range(0, MXU_N, LANES):
                k_rot = rope(r[:, k:k + LANES])
                put(dk_ref, c + k, k_rot)
                put_state(kst_ref, c + k, k_rot)
        c0 += w_dk
        for c in range(0, w_dv, MXU_N):
            r = mm(c0 + c, MXU_N)
            put(dv_ref, c, r)
            for k in range(0, MXU_N, LANES):
                put_state(vst_ref, c + k, r[:, k:k + LANES])
        c0 += w_dv
        z = mm(c0, LANES) + bias_ref[...]
        log_sig = jnp.minimum(z, 0.0) - jnp.log1p(jnp.exp(-jnp.abs(z)))
        lane = lax.broadcasted_iota(jnp.int32, z.shape, 1)
        put(gt_ref, 0, jnp.where(lane < hm, z, jnp.where(lane < 2 * hm, log_sig, 0.0)))

    prologue(0)
    for i in range(n_sub):
        project(i)


def _project(x, mod, g_norm, w_proj, w_mv, bias, tables, *, nb, t, n_sub, hm, widths, k_scale, q_scale,
             v_transposed):
    b, s, d = x.shape
    t = t * n_sub
    cos_t, sa_t, sb_t = tables
    grid = (b // nb, s // t)
    tok = lambda width: pl.BlockSpec((nb, t, width), lambda i, j: (i, j, 0))
    tab = pl.BlockSpec((t, LANES), lambda i, j: (j, 0))
    w_mq, w_mk, w_mv_, w_mo, w_dq, w_dk, w_dv = widths
    n_heads = w_dk // LANES
    out_widths = [(w_mq, BF16), (w_mk, BF16), (w_mv_, BF16), (w_mo, BF16),
                  (w_dq, BF16), (w_dk, BF16), (w_dv, BF16), (0, F32), (0, F32), (LANES, F32)]
    out_specs = [tok(w) for w, _ in out_widths]
    out_shape = [jax.ShapeDtypeStruct((b, s, w), dt) for w, dt in out_widths]
    for i in (7, 8):
        out_specs[i] = pl.BlockSpec((nb, t * n_heads, LANES), lambda i, j: (i, j, 0))
        out_shape[i] = jax.ShapeDtypeStruct((b, s * n_heads, LANES), F32)
    if v_transposed:
        assert nb == 1
        for i, w in ((2, w_mv_), (3, w_mo)):
            out_specs[i] = pl.BlockSpec((1, w, t), lambda i, j: (i, 0, j))
            out_shape[i] = jax.ShapeDtypeStruct((b, w, s), BF16)
    return pl.pallas_call(
        functools.partial(_proj_kernel, nb=nb, t=t, n_sub=n_sub, hm=hm, widths=widths,
                          k_scale=k_scale, q_scale=q_scale, v_transposed=v_transposed),
        grid=grid,
        in_specs=[tok(d), pl.BlockSpec((nb, N_ADA, d), lambda i, j: (i, 0, 0)),
                  _resident(g_norm.shape), _resident(w_proj.shape), _resident(w_mv.shape),
                  _resident(bias.shape), tab, tab, tab],
        out_specs=out_specs,
        out_shape=out_shape,
        scratch_shapes=[pltpu.VMEM((nb * t, d), BF16)],
        compiler_params=_params("parallel", "parallel"),
        name="in_proj",
    )(x, mod, g_norm, w_proj, w_mv, bias, cos_t, sa_t, sb_t)


def _mlstm_kernel(q_ref, k_ref, v_ref, og_ref, gt_ref, gm_ref, c0_ref, n0_ref, m0_ref,
                  y_ref, c_ref, n_ref, m_ref, *, L, hm, dk, dv):
    @pl.when(pl.program_id(1) == 0)
    def _():
        c_ref[...] = c0_ref[...]
        n_ref[...] = n0_ref[...]
        m_ref[...] = m0_ref[...]

    gates = gt_ref[0]
    row = lax.broadcasted_iota(jnp.int32, (L, L), 0)
    col = lax.broadcasted_iota(jnp.int32, (L, L), 1)
    causal = col <= row
    diag = col == row

    def as_row(col_vec):
        return jnp.sum(jnp.where(diag, col_vec, 0.0), axis=0, keepdims=True)

    for h in range(hm):
        ig_col = gates[:, h:h + 1]
        lf_row = as_row(gates[:, hm + h:hm + h + 1])
        b_col = jnp.sum(jnp.where(causal, lf_row, 0.0), axis=-1, keepdims=True)
        r_row = as_row(ig_col - b_col)
        m_prev = m_ref[0, h:h + 1, :]
        c_h = c_ref[0, h]
        n_h = n_ref[0, h:h + 1, :]
        q_h = q_ref[0, :, h * dk:(h + 1) * dk]
        k_h = k_ref[0, :, h * dk:(h + 1) * dk]
        v_h = v_ref[0, :, h * dv:(h + 1) * dv]

        dmat = jnp.where(causal, b_col + r_row, -jnp.inf)
        inter = b_col + m_prev
        m_t = jnp.maximum(inter, jnp.max(dmat, axis=-1, keepdims=True))
        w_inter = jnp.exp(inter - m_t)
        s = _dot_nt(q_h, k_h) * jnp.exp(dmat - m_t)
        num = _dot(s.astype(BF16), v_h) + _dot_nt(q_h, c_h.astype(BF16)) * w_inter
        qn = jnp.sum(q_h.astype(F32) * n_h, axis=-1, keepdims=True)
        den = jnp.sum(s, axis=-1, keepdims=True) + w_inter * qn
        den = jnp.maximum(jnp.abs(den), jnp.exp(-m_t))
        hid = num / den
        y = _rms(hid) * gm_ref[h:h + 1, :] * og_ref[0, :, h * dv:(h + 1) * dv].astype(F32)
        y_ref[0, :, h * dv:(h + 1) * dv] = y.astype(y_ref.dtype)

        m_last = m_t[L - 1:L, :]
        b_last = b_col[L - 1:L, :]
        w_state = jnp.exp(b_last + m_prev - m_last)
        w_rows = jnp.exp(b_last - b_col + ig_col - m_last)
        kw = k_h.astype(F32) * w_rows
        c_ref[0, h] = w_state * c_h + _dot_tn(v_h, kw.astype(BF16))
        n_ref[0, h:h + 1, :] = w_state * n_h + jnp.sum(kw, axis=0, keepdims=True)
        m_ref[0, h:h + 1, :] = m_last


def _mlstm_t_kernel(q_ref, k_ref, vt_ref, og_ref, gt_ref, gmt_ref, y_ref, c_ref, n_ref, m_ref, st_ref,
                    *, rows, L, hm, dk, dv):
    ci = pl.program_id(1)

    @pl.when(ci == 0)
    def _():
        st_ref[...] = jnp.zeros_like(st_ref)
        m_ref[...] = jnp.zeros_like(m_ref)

    src = lax.broadcasted_iota(jnp.int32, (L, L), 0)
    tgt = lax.broadcasted_iota(jnp.int32, (L, L), 1)
    visible = src <= tgt
    lower = (tgt <= src).astype(BF16)
    ones_rows = (lax.broadcasted_iota(jnp.int32, (BF16_ROWS, L), 0) == 0).astype(BF16)

    gates_all = jnp.concatenate([gt_ref[r] for r in range(rows)], axis=-1)
    g1 = gates_all.astype(BF16)
    r1 = gates_all - g1.astype(F32)
    g2 = r1.astype(BF16)
    g3 = (r1 - g2.astype(F32)).astype(BF16)
    csum_all = _dot(lower, g1) + _dot(lower, g2) + _dot(lower, g3)

    row_vals = {}

    def decay_stage(r, h):
        if r not in row_vals:
            csum = csum_all[:, r * LANES:(r + 1) * LANES]
            row_vals[r] = (gt_ref[r], csum, csum.T)
        gates, csum, csum_t = row_vals[r]
        r_col = gates[:, h:h + 1] - csum[:, hm + h:hm + h + 1]
        r_lanes = jnp.broadcast_to(r_col, (L, LANES))
        b_row = csum_t[hm + h:hm + h + 1, :]
        m_prev = m_ref[r, h:h + 1, :]
        dmat = jnp.where(visible, jnp.concatenate([r_lanes] * (L // LANES), axis=-1) + b_row, -jnp.inf)
        inter = b_row + m_prev
        m_t = jnp.maximum(inter, jnp.max(dmat, axis=0, keepdims=True))
        m_last = m_t[:, L - 1:L]
        b_last = b_row[:, L - 1:L]
        return dict(
            w_intra=jnp.exp(dmat - m_t), w_inter=jnp.exp(inter - m_t), floor=jnp.exp(-m_t),
            w_state=jnp.exp(b_last + m_prev - m_last), w_rows=jnp.exp(r_lanes[:, :dk] + (b_last - m_last)),
            m_last=m_last)

    def matmul_stage(r, h, w):
        state = st_ref[r, h]
        q_h = q_ref[r, :, h * dk:(h + 1) * dk]
        k_h = k_ref[r, :, h * dk:(h + 1) * dk]
        vt_h = vt_ref[r, h * dv:(h + 1) * dv, :]
        s = _dot_nt(k_h, q_h) * w["w_intra"]
        sq = _dot_nt(state.astype(BF16), q_h)
        num = _dot(vt_h, s.astype(BF16)) + sq[:dv, :] * w["w_inter"]
        den = jnp.sum(s, axis=0, keepdims=True) + w["w_inter"] * sq[dv:dv + 1, :]
        hid = num / jnp.maximum(jnp.abs(den), w["floor"])
        scale = lax.rsqrt(jnp.mean(hid * hid, axis=0, keepdims=True) + NORM_EPS)
        gain = jnp.concatenate([gmt_ref[:, h * LANES:(h + 1) * LANES]] * (L // LANES), axis=-1)
        y_t = hid * scale * gain * og_ref[r, h * dv:(h + 1) * dv, :].astype(F32)
        y_ref[r, h * dv:(h + 1) * dv, :] = y_t.astype(y_ref.dtype)

        kw = (k_h.astype(F32) * w["w_rows"]).astype(BF16)
        vt_aug = jnp.concatenate([vt_h, ones_rows], axis=0)
        st_ref[r, h] = w["w_state"] * state + _dot(vt_aug, kw)
        m_ref[r, h:h + 1, :] = w["m_last"]

    chains = [(r, h) for r in range(rows) for h in range(hm)]
    pending = None
    for chain in chains + [None]:
        weights = decay_stage(*chain) if chain is not None else None
        if pending is not None:
            matmul_stage(*pending)
        pending = chain + (weights,) if chain is not None else None

    @pl.when(ci == pl.num_programs(1) - 1)
    def _():
        for r in range(rows):
            for h in range(hm):
                c_ref[r, h] = st_ref[r, h, :dv, :]
                n_ref[r, h:h + 1, :] = st_ref[r, h, dv:dv + 1, :]


def _mlstm(mq, mk, mv, og, gates, g_mlstm, *, L, rows=1, init=None):
    b, s, _ = mq.shape
    hm, dv = g_mlstm.shape
    dk = mq.shape[-1] // hm
    assert init is None or rows == 1
    tok = lambda width: pl.BlockSpec((rows, L, width), lambda i, j: (i, j, 0))
    st_c = pl.BlockSpec((rows, hm, dv, dk), lambda i, j: (i, 0, 0, 0))
    st_n = pl.BlockSpec((rows, hm, dk), lambda i, j: (i, 0, 0))
    st_m = pl.BlockSpec((rows, hm, 1), lambda i, j: (i, 0, 0))
    out_specs = [tok(hm * dv), st_c, st_n, st_m]
    out_shape = [jax.ShapeDtypeStruct((b, s, hm * dv), BF16),
                 jax.ShapeDtypeStruct((b, hm, dv, dk), F32),
                 jax.ShapeDtypeStruct((b, hm, dk), F32),
                 jax.ShapeDtypeStruct((b, hm, 1), F32)]
    if init is not None:
        return pl.pallas_call(
            functools.partial(_mlstm_kernel, L=L, hm=hm, dk=dk, dv=dv),
            grid=(b, s // L),
            in_specs=[tok(hm * dk), tok(hm * dk), tok(hm * dv), tok(hm * dv), tok(LANES),
                      _resident(g_mlstm.shape), st_c, st_n, st_m],
            out_specs=out_specs,
            out_shape=out_shape,
            compiler_params=_params("parallel", "arbitrary"),
            name="mlstm_init",
        )(mq, mk, mv, og, gates, g_mlstm, *init)
    assert L % LANES == 0
    g_t = jnp.broadcast_to(g_mlstm.T[:, :, None], (dv, hm, LANES)).reshape(dv, hm * LANES)
    feat = pl.BlockSpec((rows, hm * dv, L), lambda i, j: (i, 0, j))
    out_specs[0] = feat
    out_shape[0] = jax.ShapeDtypeStruct((b, hm * dv, s), BF16)
    return pl.pallas_call(
        functools.partial(_mlstm_t_kernel, rows=rows, L=L, hm=hm, dk=dk, dv=dv),
        grid=(b // rows, s // L),
        in_specs=[tok(hm * dk), tok(hm * dk), feat, feat, tok(LANES), _resident(g_t.shape)],
        out_specs=out_specs,
        out_shape=out_shape,
        scratch_shapes=[pltpu.VMEM((rows, hm, dv + BF16_ROWS, dk), F32)],
        compiler_params=_params("parallel", "arbitrary"),
        name="mlstm",
    )(mq, mk, mv, og, gates, g_t)


def _diff_lambda(lam_ref, layer):
    lq = lam_ref[...]
    a = jnp.sum(lq[0:1, :] * lq[1:2, :], axis=-1, keepdims=True)
    b = jnp.sum(lq[2:3, :] * lq[3:4, :], axis=-1, keepdims=True)
    lam_init = 0.8 - 0.6 * float(np.exp(-0.3 * layer))
    return jnp.exp(a) - jnp.exp(b) + lam_init, lam_init


def _split_maps(q):
    lane = lax.broadcasted_iota(jnp.int32, q.shape, 1)
    zero = jnp.zeros_like(q)
    half = q.shape[-1] // 2
    return jnp.where(lane < half, q, zero), jnp.where(lane >= half, q, zero)


def _attn_prompt_kernel(q_ref, k_ref, v_ref, lam_ref, g_ref, o_ref, vt_ref, s_ref, p_ref, *, tq, layer):
    s_len = k_ref.shape[1]
    n_heads, dv = g_ref.shape[0], g_ref.shape[-1]
    n_blk = s_len // tq
    for hh in range(n_heads):
        vt_ref[hh, dv:, :] = (lax.broadcasted_iota(jnp.int32, (BF16_ROWS, s_len), 0) == 0).astype(BF16)

    def lanes(hh):
        return slice(hh * dv, (hh + 1) * dv)

    def transpose_values(hh, j):
        v_blk = v_ref[0, j * tq:(j + 1) * tq, lanes(hh)]
        vt_ref[hh, :dv, j * tq:(j + 1) * tq] = v_blk.astype(F32).T.astype(BF16)

    lam, lam_init = _diff_lambda(lam_ref, layer)
    key_chunk = lax.broadcasted_iota(jnp.int32, (tq, tq), 0) // CHUNK
    qry_chunk = lax.broadcasted_iota(jnp.int32, (tq, tq), 1) // CHUNK
    visible = key_chunk <= qry_chunk

    def score_block(slot, hh, q_map, qi, j, m8):
        s = _dot_nt(k_ref[0, j * tq:(j + 1) * tq, lanes(hh)], q_map)
        if j == qi:
            s = jnp.where(visible, s, -jnp.inf)
        s_ref[slot, j] = s
        m_j = jnp.max(s.reshape(tq // 8, 8, tq), axis=0)
        return m_j if m8 is None else jnp.maximum(m8, m_j)

    def weight_block(slot, j, m):
        p_ref[slot, j * tq:(j + 1) * tq, :] = jnp.exp2(s_ref[slot, j] - m).astype(BF16)

    stages = [(hh, qi, c) for hh in range(n_heads) for qi in reversed(range(n_blk)) for c in range(2)]
    prev = None
    outs = {}
    for k, stage in enumerate(stages + [None]):
        n_score = 0
        if stage is not None:
            hh, qi, c = stage
            if c == 0:
                q_maps = _split_maps(q_ref[0, qi * tq:(qi + 1) * tq, lanes(hh)])
            n_score = qi + 1
        n_weight = prev[2] + 1 if prev is not None else 0
        m8 = None
        for j in range(max(n_score, n_weight)):
            if j < n_score:
                m8 = score_block(k % 2, hh, q_maps[c], qi, j, m8)
                if qi == n_blk - 1 and c == 0:
                    transpose_values(hh, j)
            if j < n_weight:
                weight_block(prev[0] % 2, j, prev[4])
        if prev is not None:
            pk, ph, pqi, pc, _ = prev
            n_keys = (pqi + 1) * tq
            acc = _dot(vt_ref[ph, :, :n_keys], p_ref[pk % 2, :n_keys, :])
            outs[pc] = acc[:dv, :] / acc[dv:dv + 1, :]
            if pc == 1:
                o = (outs[0] - lam * outs[1]).T
                o_ref[0, pqi * tq:(pqi + 1) * tq, lanes(ph)] = (
                    _rms(o) * g_ref[ph] * (1.0 - lam_init)).astype(o_ref.dtype)
        if stage is not None:
            prev = (k, hh, qi, c, jnp.max(m8, axis=0, keepdims=True))


def _attn_prompt(dq, dk, dv, lambda_qk, g_diff, *, tq, heads, layer):
    b, s, _ = dq.shape
    hd, dvh = g_diff.shape
    n_blk = s // tq
    head = pl.BlockSpec((1, s, heads * dvh), lambda i, h: (i, 0, h))
    return pl.pallas_call(
        functools.partial(_attn_prompt_kernel, tq=tq, layer=layer),
        grid=(b, hd // heads),
        in_specs=[head, head, head, _resident(lambda_qk.shape),
                  pl.BlockSpec((heads, 1, dvh), lambda i, h: (h, 0, 0))],
        out_specs=head,
        out_shape=jax.ShapeDtypeStruct((b, s, hd * dvh), BF16),
        scratch_shapes=[pltpu.VMEM((heads, dvh + BF16_ROWS, s), BF16),
                        pltpu.VMEM((2, n_blk, tq, tq), F32), pltpu.VMEM((2, s, tq), BF16)],
        compiler_params=_params("parallel", "parallel"),
        name="diff_attn_prompt",
    )(dq, dk, dv, lambda_qk, g_diff.reshape(hd, 1, dvh))


def _attn_sample_kernel(q_ref, kc_ref, vc_ref, kn_ref, vn_ref, lam_ref, g_ref, o_ref, *, hd, layer):
    lam, lam_init = _diff_lambda(lam_ref, layer)
    past = kc_ref.shape[1] // hd
    dvh = g_ref.shape[-1]
    for h in range(hd):
        cols = slice(h * dvh, (h + 1) * dvh)
        q_maps = _split_maps(q_ref[0, :, cols])
        k_c = kc_ref[0, pl.ds(h, past, stride=hd), :].astype(BF16)
        v_c = vc_ref[0, pl.ds(h, past, stride=hd), :].astype(BF16)
        k_n = kn_ref[0, :, cols]
        v_n = vn_ref[0, :, cols]
        outs = []
        for c in range(2):
            s_c = _dot_nt(q_maps[c], k_c)
            s_n = _dot_nt(q_maps[c], k_n)
            m = jnp.maximum(jnp.max(s_c, axis=-1, keepdims=True), jnp.max(s_n, axis=-1, keepdims=True))
            p_c = jnp.exp2(s_c - m)
            p_n = jnp.exp2(s_n - m)
            l = jnp.sum(p_c, axis=-1, keepdims=True) + jnp.sum(p_n, axis=-1, keepdims=True)
            outs.append((_dot(p_c.astype(BF16), v_c) + _dot(p_n.astype(BF16), v_n)) / l)
        o = outs[0] - lam * outs[1]
        o_ref[0, :, cols] = (_rms(o) * g_ref[h:h + 1, :] * (1.0 - lam_init)).astype(o_ref.dtype)


def _attn_sample(dq, dk, dv, cache_k, cache_v, lambda_qk, g_diff, *, layer):
    b, t, width = dq.shape
    hd, dvh = g_diff.shape
    new = pl.BlockSpec((1, t, width), lambda i: (i, 0, 0))
    old = pl.BlockSpec((1, cache_k.shape[1], dvh), lambda i: (i, 0, 0))
    return pl.pallas_call(
        functools.partial(_attn_sample_kernel, hd=hd, layer=layer),
        grid=(b,),
        in_specs=[new, old, old, new, new, _resident(lambda_qk.shape), _resident(g_diff.shape)],
        out_specs=new,
        out_shape=jax.ShapeDtypeStruct((b, t, width), BF16),
        compiler_params=_params("parallel"),
        name="diff_attn_sample",
    )(dq, cache_k, cache_v, dk, dv, lambda_qk, g_diff)


def _interleave_kernel(g_ref, u_ref, o_ref):
    for i in range(g_ref.shape[-1] // LANES):
        o_ref[:, 2 * i * LANES:(2 * i + 1) * LANES] = g_ref[0, :, i * LANES:(i + 1) * LANES].astype(BF16)
        o_ref[:, (2 * i + 1) * LANES:(2 * i + 2) * LANES] = u_ref[0, :, i * LANES:(i + 1) * LANES].astype(BF16)


def _interleave_gate_up(w_in_all, layer):
    _, d, two_f = w_in_all.shape
    f = two_f // 2
    n_groups = f // LANES
    per_step = 2 if n_groups % 2 == 0 else 1
    tw = per_step * LANES
    n_steps = n_groups // per_step
    return pl.pallas_call(
        _interleave_kernel,
        grid=(n_steps,),
        in_specs=[pl.BlockSpec((1, d, tw), lambda j: (layer, 0, j)),
                  pl.BlockSpec((1, d, tw), lambda j: (layer, 0, n_steps + j))],
        out_specs=pl.BlockSpec((d, 2 * tw), lambda j: (0, j)),
        out_shape=jax.ShapeDtypeStruct((d, two_f), BF16),
        compiler_params=_params("parallel"),
        name="interleave_gate_up",
    )(w_in_all, w_in_all)


def _rope_tables(pos, dqk):
    rope_dim = dqk // 4
    half = rope_dim // 2
    inv_freq = ROPE_THETA ** (-jnp.arange(half, dtype=F32) * (2.0 / rope_dim))
    ang = pos.astype(F32)[:, None] * inv_freq[None, :]
    cos, sin = jnp.cos(ang), jnp.sin(ang)
    n = pos.shape[0]
    pad = jnp.zeros((n, dqk - rope_dim), F32)
    zero = jnp.zeros((n, half), F32)
    cos_t = jnp.concatenate([cos, cos, pad + 1.0], axis=1)
    sa_t = jnp.concatenate([-sin, zero, pad], axis=1)
    sb_t = jnp.concatenate([zero, sin, pad], axis=1)
    rep = LANES // dqk
    return tuple(jnp.tile(a, (1, rep)) for a in (cos_t, sa_t, sb_t))


def _layer(x, c_mod, pos, attend, mlstm_init, weights, dims, *, nb, t, n_sub, L, rows, layer):
    (g_norm, w1_in, w1_out, w_proj, w_vo, bias, g_mlstm, g_diff, lambda_qk, w_o, w2_in, w2_out) = weights
    hm, dk_m, widths = dims
    v_transposed = mlstm_init is None
    x1 = _ffn_block(x, c_mod, g_norm, w1_in, w1_out, nb=nb, t=t, n_sub=n_sub, g_idx=(0, 1), m_idx=(0, 1, 2))
    dqk = g_diff.shape[1] // 2
    mq, mk, mv, og, dq, dk, dv, k_state, v_state, gates = _project(
        x1, c_mod, g_norm, w_proj, w_vo.T if v_transposed else w_vo, bias, _rope_tables(pos, dqk),
        nb=nb, t=t, n_sub=n_sub, hm=hm, widths=widths, k_scale=dk_m ** -0.5, q_scale=dqk ** -0.5 * LOG2_E,
        v_transposed=v_transposed)
    y_m, c_new, n_new, m_new = _mlstm(mq, mk, mv, og, gates, g_mlstm, L=L, rows=rows, init=mlstm_init)
    y_d = attend(dq, dk, dv)
    y = _ffn_block(x1, c_mod, g_norm, w2_in, w2_out, nb=nb, t=t, n_sub=n_sub, g_idx=(4, 5), m_idx=(6, 7, 8),
                   heads=(y_m, y_d), w_o=w_o, ym_transposed=v_transposed)
    return y, (k_state, v_state, c_new, n_new, m_new)


def kernel(x_prompt, x_sample, cache_k, cache_v, state_C, state_n, state_m, c_prompt, c_sample,
           w_ada, b_ada, g_norm, w_ffn1_in, w_ffn1_out, w_in, b_igate, b_fgate, g_mlstm, g_diff,
           lambda_qk, w_out, w_ffn2_in, w_ffn2_out):
    depth = w_ada.shape[0]
    bp, sp, d = x_prompt.shape
    bs, ts, _ = x_sample.shape
    past = cache_k.shape[2]
    hm, mv_dim = g_mlstm.shape[1:]
    hd, dv_dim = g_diff.shape[1:]
    mqk = mv_dim // 2
    dqk = dv_dim // 2
    sizes = [hm * mqk, hm * mqk, hm * mv_dim, hm * mv_dim, hm, hm, hd * 2 * dqk, hd * 2 * dqk, hd * dv_dim]
    offs = np.concatenate([[0], np.cumsum(sizes)])
    widths = tuple(sizes[i] for i in (0, 1, 2, 3, 6, 7, 8))
    t_prompt = min(512, sp)
    sub_prompt = 2 if sp % (2 * t_prompt) == 0 else 1
    l_prompt = min(256, sp)
    rows_prompt = max(r for r in (4, 2, 1) if bp % r == 0)

    y_p, y_s = x_prompt, x_sample
    states_p, states_s = [], []
    for l in range(depth):
        mod = _ada(jnp.concatenate([c_prompt, c_sample], axis=0), w_ada[l], b_ada[l])
        mod = mod.reshape(bp + bs, N_ADA, d)
        cols = [w_in[l][:, offs[i]:offs[i + 1]] for i in range(9)]
        gate_cols = jnp.concatenate([cols[4], cols[5], jnp.zeros((d, LANES - 2 * hm), F32)], axis=1)
        w_proj = jnp.concatenate([cols[i] for i in (0, 1, 6, 7, 8)] + [gate_cols], axis=1).astype(BF16)
        w_vo = jnp.concatenate([cols[2], cols[3]], axis=1).astype(BF16)
        bias = jnp.concatenate([b_igate[l], b_fgate[l], jnp.zeros((LANES - 2 * hm,), F32)]).reshape(1, LANES)
        weights = (g_norm[l], _interleave_gate_up(w_ffn1_in, l), w_ffn1_out[l].astype(BF16), w_proj,
                   w_vo, bias, g_mlstm[l], g_diff[l], lambda_qk[l], w_out[l].astype(BF16),
                   _interleave_gate_up(w_ffn2_in, l), w_ffn2_out[l].astype(BF16))
        dims = (hm, mqk, widths)

        attend_p = functools.partial(_attn_prompt, lambda_qk=lambda_qk[l], g_diff=g_diff[l],
                                     tq=min(256, sp), heads=2 if hd % 2 == 0 else 1, layer=l)
        y_p, st_p = _layer(y_p, mod[:bp], jnp.arange(sp), attend_p, None, weights, dims,
                           nb=1, t=t_prompt, n_sub=sub_prompt, L=l_prompt, rows=rows_prompt, layer=l)

        ck = cache_k[l].reshape(bs, past * hd, 2 * dqk)
        cv = cache_v[l].reshape(bs, past * hd, dv_dim)
        attend_s = functools.partial(_attn_sample, cache_k=ck, cache_v=cv, lambda_qk=lambda_qk[l],
                                     g_diff=g_diff[l], layer=l)
        init_s = (state_C[l], state_n[l], state_m[l].reshape(bs, hm, 1))
        y_s, st_s = _layer(y_s, mod[bp:], past + jnp.arange(ts), attend_s, init_s, weights, dims,
                           nb=bs, t=ts, n_sub=1, L=ts, rows=1, layer=l)
        states_p.append(st_p)
        states_s.append(st_s)

    def stack(states, b, s):
        if depth == 1:
            k, v, c, n, m = [a[None] for a in states[0]]
        else:
            k, v, c, n, m = [jnp.stack(a, axis=0) for a in zip(*states)]
        return (k.reshape(depth, b, s, hd, 2 * dqk), v.reshape(depth, b, s, hd, dv_dim),
                c, n, m.reshape(depth, b, hm))

    return (y_p, y_s) + stack(states_p, bp, sp) + stack(states_s, bs, ts)
```

```python
import functools

import jax
import jax.numpy as jnp
import numpy as np
from jax import lax
from jax.experimental import pallas as pl
from jax.experimental.pallas import tpu as pltpu

F32 = jnp.float32
BF16 = jnp.bfloat16

LANES = 128
BF16_ROWS = 16
MXU_N = 256
VMEM_LIMIT_BYTES = 56 << 20
NORM_EPS = 1e-6
ROPE_THETA = 500000.0
N_ADA = 9
CHUNK = 64
LOG2_E = 1.4426950408889634

_NT = (((1,), (1,)), ((), ()))
_TN = (((0,), (0,)), ((), ()))


def _dot(a, b):
    return jnp.dot(a, b, preferred_element_type=F32)


def _dot_nt(a, b):
    return lax.dot_general(a, b, _NT, preferred_element_type=F32)


def _dot_tn(a, b):
    return lax.dot_general(a, b, _TN, preferred_element_type=F32)


def _rms(x):
    return x * lax.rsqrt(jnp.mean(x * x, axis=-1, keepdims=True) + NORM_EPS)


def _silu(x):
    return x * jax.nn.sigmoid(x)


def _params(*sem):
    return pltpu.CompilerParams(dimension_semantics=sem, vmem_limit_bytes=VMEM_LIMIT_BYTES)


def _resident(shape):
    nd = len(shape)
    return pl.BlockSpec(shape, lambda *_: (0,) * nd, pipeline_mode=pl.Buffered(1))


def _ada_kernel(c_ref, w_ref, b_ref, o_ref):
    a = _silu(c_ref[...]).astype(BF16)
    o_ref[...] = _dot(a, w_ref[...].astype(BF16)) + b_ref[...]


def _ada(c, w_ada, b_ada):
    n, d = c.shape
    width = w_ada.shape[1]
    tn = width // 8
    return pl.pallas_call(
        _ada_kernel,
        grid=(width // tn,),
        in_specs=[pl.BlockSpec((n, d), lambda j: (0, 0)),
                  pl.BlockSpec((d, tn), lambda j: (0, j)),
                  pl.BlockSpec((1, tn), lambda j: (0, j))],
        out_specs=pl.BlockSpec((n, tn), lambda j: (0, j)),
        out_shape=jax.ShapeDtypeStruct((n, width), F32),
        compiler_params=_params("parallel"),
        name="ada",
    )(c, w_ada, b_ada.reshape(1, width))


def _ffn_kernel(*refs, nb, t, n_sub, pre_proj, ym_transposed, g_idx, m_idx):
    if pre_proj:
        x_ref, ym_ref, yd_ref, mod_ref, g_ref, wo_ref, w_in_ref, w_out_ref, o_ref, h_ref, act_ref = refs
    else:
        x_ref, mod_ref, g_ref, w_in_ref, w_out_ref, o_ref, h_ref, act_ref = refs
    d = x_ref.shape[-1]
    ts = t // n_sub
    rows = nb * ts
    g_a, g_b = g_idx
    i_shift, i_scale, i_gate = m_idx
    n_groups = w_out_ref.shape[0] // LANES

    def mrow(i):
        return mod_ref[:, i:i + 1, :]

    def grow(i):
        return g_ref[i:i + 1, :][None]

    def head_proj(i):
        tok = slice(i * ts, (i + 1) * ts)
        if ym_transposed:
            w_m = ym_ref.shape[1]
            return _dot_tn(ym_ref[0, :, tok], wo_ref[:w_m, :]) + _dot(yd_ref[0, tok, :], wo_ref[w_m:, :])
        heads = jnp.concatenate([ym_ref[:, tok, :], yd_ref[:, tok, :]], axis=-1).reshape(rows, d)
        return _dot(heads, wo_ref[...])

    def prologue(i, o=None):
        tok = slice(i * ts, (i + 1) * ts)
        x = x_ref[:, tok, :]
        if pre_proj:
            o = head_proj(i) if o is None else o
            x = x + mrow(5) * (_rms(o.reshape(nb, ts, d)) * grow(3))
        h = _rms(x) * grow(g_a) * (1.0 + mrow(i_scale)) + mrow(i_shift)
        h_ref[i * rows:(i + 1) * rows, :] = h.astype(BF16).reshape(rows, d)
        o_ref[:, tok, :] = x

    def chunk(i, j):
        gu = _dot(h_ref[i * rows:(i + 1) * rows, :], w_in_ref[:, j * MXU_N:(j + 1) * MXU_N])
        act = _silu(gu[:, :LANES]) * gu[:, LANES:]
        act_ref[i * rows:(i + 1) * rows, j * LANES:(j + 1) * LANES] = act.astype(BF16)

    def epilogue(tok, out):
        n_tok = tok.stop - tok.start
        o_ref[:, tok, :] = o_ref[:, tok, :] + (0.5 * mrow(i_gate)) * (_rms(out.reshape(nb, n_tok, d)) * grow(g_b))

    early = {i: head_proj(i) for i in range(min(n_sub, 2))} if pre_proj else {}
    prologue(0, early.get(0))
    waiting = None
    for i in range(n_sub):
        for j in range(n_groups):
            chunk(i, j)
            if j == 1:
                if i + 1 < n_sub:
                    prologue(i + 1, early.get(i + 1))
                if waiting is not None:
                    epilogue(*waiting)
                    waiting = None
        parts = 2 if (i == n_sub - 1 and nb == 1 and ts % (2 * BF16_ROWS) == 0) else 1
        step = ts // parts
        for p in range(parts):
            r0 = i * rows + p * step
            out = _dot(act_ref[r0:r0 + step * nb, :], w_out_ref[...])
            if waiting is not None:
                epilogue(*waiting)
            waiting = (slice(i * ts + p * step, i * ts + (p + 1) * step), out)
    epilogue(*waiting)


def _ffn_block(x, mod, g_norm, w_in_r, w_out, *, nb, t, n_sub, g_idx, m_idx, heads=None, w_o=None,
               ym_transposed=False):
    b, s, d = x.shape
    t = t * n_sub
    f = w_out.shape[0]
    pre_proj = heads is not None
    grid = (b // nb, s // t)
    tok = lambda width: pl.BlockSpec((nb, t, width), lambda i, j: (i, j, 0))
    in_specs = [tok(d)]
    args = [x]
    if pre_proj:
        ym, yd = heads
        if ym_transposed:
            assert nb == 1
            ym_spec = pl.BlockSpec((1, ym.shape[1], t), lambda i, j: (i, 0, j))
        else:
            ym_spec = tok(ym.shape[-1])
        in_specs += [ym_spec, tok(yd.shape[-1])]
        args += [ym, yd]
    in_specs += [pl.BlockSpec((nb, N_ADA, d), lambda i, j: (i, 0, 0)), _resident(g_norm.shape)]
    args += [mod, g_norm]
    if pre_proj:
        in_specs.append(_resident(w_o.shape))
        args.append(w_o)
    in_specs += [_resident(w_in_r.shape), _resident(w_out.shape)]
    args += [w_in_r, w_out]
    return pl.pallas_call(
        functools.partial(_ffn_kernel, nb=nb, t=t, n_sub=n_sub, pre_proj=pre_proj, ym_transposed=ym_transposed,
                          g_idx=g_idx, m_idx=m_idx),
        grid=grid,
        in_specs=in_specs,
        out_specs=tok(d),
        out_shape=jax.ShapeDtypeStruct((b, s, d), F32),
        scratch_shapes=[pltpu.VMEM((nb * t, d), BF16), pltpu.VMEM((nb * t, f), BF16)],
        compiler_params=_params("parallel", "parallel"),
        name="ffn_post" if pre_proj else "ffn_pre",
    )(*args)


def _proj_kernel(x_ref, mod_ref, g_ref, w_ref, wv_ref, bias_ref, cos_ref, sa_ref, sb_ref,
                 mq_ref, mk_ref, mv_ref, og_ref, dq_ref, dk_ref, dv_ref, kst_ref, vst_ref, gt_ref, h_ref,
                 *, nb, t, n_sub, hm, widths, k_scale, q_scale, v_transposed):
    d = x_ref.shape[-1]
    ts = t // n_sub
    rows = nb * ts
    w_mq, w_mk, w_mv, w_mo, w_dq, w_dk, w_dv = widths
    n_heads = w_dk // LANES

    def prologue(i):
        tok = slice(i * ts, (i + 1) * ts)
        h = _rms(x_ref[:, tok, :]) * g_ref[2:3, :][None] * (1.0 + mod_ref[:, 4:5, :]) + mod_ref[:, 3:4, :]
        h_ref[i * rows:(i + 1) * rows, :] = h.astype(BF16).reshape(rows, d)

    def project(i):
        tok = slice(i * ts, (i + 1) * ts)
        h_rows = slice(i * rows, (i + 1) * rows)

        def mm(c0, width):
            return _dot(h_ref[h_rows, :], w_ref[:, c0:c0 + width])

        def put(ref, c, val):
            width = val.shape[-1]
            ref[:, tok, c:c + width] = val.astype(ref.dtype).reshape(nb, ts, width)

        def put_state(ref, c, val):
            ref[:, pl.ds(i * ts * n_heads + c // LANES, ts, stride=n_heads), :] = val.reshape(nb, ts, LANES)

        def rope(xb):
            x3 = xb.reshape(nb, ts, LANES)
            up = pltpu.roll(xb, LANES - 8, 1).reshape(nb, ts, LANES)
            dn = pltpu.roll(xb, 8, 1).reshape(nb, ts, LANES)
            r = x3 * cos_ref[tok, :][None] + up * sa_ref[tok, :][None] + dn * sb_ref[tok, :][None]
            return r.reshape(rows, LANES)

        c0 = 0
        for c in range(0, w_mq, MXU_N):
            put(mq_ref, c, mm(c0 + c, min(MXU_N, w_mq - c)))
        c0 += w_mq
        if i + 1 < n_sub:
            prologue(i + 1)
        for c in range(0, w_mk, MXU_N):
            put(mk_ref, c, mm(c0 + c, min(MXU_N, w_mk - c)) * k_scale)
        c0 += w_mk
        for c in range(0, w_mv, MXU_N):
            if v_transposed:
                mv_ref[0, c:c + MXU_N, tok] = _dot_nt(wv_ref[c:c + MXU_N, :], h_ref[h_rows, :]).astype(mv_ref.dtype)
            else:
                put(mv_ref, c, _dot(h_ref[h_rows, :], wv_ref[:, c:c + MXU_N]))
        for c in range(0, w_mo, MXU_N):
            if v_transposed:
                pre = _dot_nt(wv_ref[w_mv + c:w_mv + c + MXU_N, :], h_ref[h_rows, :])
                og_ref[0, c:c + MXU_N, tok] = jax.nn.sigmoid(pre).astype(og_ref.dtype)
            else:
                put(og_ref, c, jax.nn.sigmoid(_dot(h_ref[h_rows, :], wv_ref[:, w_mv + c:w_mv + c + MXU_N])))
        for c in range(0, w_dq, MXU_N):
            r = mm(c0 + c, MXU_N)
            for k in range(0, MXU_N, LANES):
                put(dq_ref, c + k, rope(r[:, k:k + LANES]) * q_scale)
        c0 += w_dq
        for c in range(0, w_dk, MXU_N):
            r = mm(c0 + c, MXU_N)
            for k in range(0, MXU_N, LANES):
                k_rot = rope(r[:, k:k + LANES])
                put(dk_ref, c + k, k_rot)
                put_state(kst_ref, c + k, k_rot)
        c0 += w_dk
        for c in range(0, w_dv, MXU_N):
            r = mm(c0 + c, MXU_N)
            put(dv_ref, c, r)
            for k in range(0, MXU_N, LANES):
                put_state(vst_ref, c + k, r[:, k:k + LANES])
        c0 += w_dv
        z = mm(c0, LANES) + bias_ref[...]
        log_sig = jnp.minimum(z, 0.0) - jnp.log1p(jnp.exp(-jnp.abs(z)))
        lane = lax.broadcasted_iota(jnp.int32, z.shape, 1)
        put(gt_ref, 0, jnp.where(lane < hm, z, jnp.where(lane < 2 * hm, log_sig, 0.0)))

    prologue(0)
    for i in range(n_sub):
        project(i)


def _project(x, mod, g_norm, w_proj, w_mv, bias, tables, *, nb, t, n_sub, hm, widths, k_scale, q_scale,
             v_transposed):
    b, s, d = x.shape
    t = t * n_sub
    cos_t, sa_t, sb_t = tables
    grid = (b // nb, s // t)
    tok = lambda width: pl.BlockSpec((nb, t, width), lambda i, j: (i, j, 0))
    tab = pl.BlockSpec((t, LANES), lambda i, j: (j, 0))
    w_mq, w_mk, w_mv_, w_mo, w_dq, w_dk, w_dv = widths
    n_heads = w_dk // LANES
    out_widths = [(w_mq, BF16), (w_mk, BF16), (w_mv_, BF16), (w_mo, BF16),
                  (w_dq, BF16), (w_dk, BF16), (w_dv, BF16), (0, F32), (0, F32), (LANES, F32)]
    out_specs = [tok(w) for w, _ in out_widths]
    out_shape = [jax.ShapeDtypeStruct((b, s, w), dt) for w, dt in out_widths]
    for i in (7, 8):
        out_specs[i] = pl.BlockSpec((nb, t * n_heads, LANES), lambda i, j: (i, j, 0))
        out_shape[i] = jax.ShapeDtypeStruct((b, s * n_heads, LANES), F32)
    if v_transposed:
        assert nb == 1
        for i, w in ((2, w_mv_), (3, w_mo)):
            out_specs[i] = pl.BlockSpec((1, w, t), lambda i, j: (i, 0, j))
            out_shape[i] = jax.ShapeDtypeStruct((b, w, s), BF16)
    return pl.pallas_call(
        functools.partial(_proj_kernel, nb=nb, t=t, n_sub=n_sub, hm=hm, widths=widths,
                          k_scale=k_scale, q_scale=q_scale, v_transposed=v_transposed),
        grid=grid,
        in_specs=[tok(d), pl.BlockSpec((nb, N_ADA, d), lambda i, j: (i, 0, 0)),
                  _resident(g_norm.shape), _resident(w_proj.shape), _resident(w_mv.shape),
                  _resident(bias.shape), tab, tab, tab],
        out_specs=out_specs,
        out_shape=out_shape,
        scratch_shapes=[pltpu.VMEM((nb * t, d), BF16)],
        compiler_params=_params("parallel", "parallel"),
        name="in_proj",
    )(x, mod, g_norm, w_proj, w_mv, bias, cos_t, sa_t, sb_t)


def _mlstm_kernel(q_ref, k_ref, v_ref, og_ref, gt_ref, gm_ref, c0_ref, n0_ref, m0_ref,
                  y_ref, c_ref, n_ref, m_ref, *, L, hm, dk, dv):
    @pl.when(pl.program_id(1) == 0)
    def _():
        c_ref[...] = c0_ref[...]
        n_ref[...] = n0_ref[...]
        m_ref[...] = m0_ref[...]

    gates = gt_ref[0]
    row = lax.broadcasted_iota(jnp.int32, (L, L), 0)
    col = lax.broadcasted_iota(jnp.int32, (L, L), 1)
    causal = col <= row
    diag = col == row

    def as_row(col_vec):
        return jnp.sum(jnp.where(diag, col_vec, 0.0), axis=0, keepdims=True)

    for h in range(hm):
        ig_col = gates[:, h:h + 1]
        lf_row = as_row(gates[:, hm + h:hm + h + 1])
        b_col = jnp.sum(jnp.where(causal, lf_row, 0.0), axis=-1, keepdims=True)
        r_row = as_row(ig_col - b_col)
        m_prev = m_ref[0, h:h + 1, :]
        c_h = c_ref[0, h]
        n_h = n_ref[0, h:h + 1, :]
        q_h = q_ref[0, :, h * dk:(h + 1) * dk]
        k_h = k_ref[0, :, h * dk:(h + 1) * dk]
        v_h = v_ref[0, :, h * dv:(h + 1) * dv]

        dmat = jnp.where(causal, b_col + r_row, -jnp.inf)
        inter = b_col + m_prev
        m_t = jnp.maximum(inter, jnp.max(dmat, axis=-1, keepdims=True))
        w_inter = jnp.exp(inter - m_t)
        s = _dot_nt(q_h, k_h) * jnp.exp(dmat - m_t)
        num = _dot(s.astype(BF16), v_h) + _dot_nt(q_h, c_h.astype(BF16)) * w_inter
        qn = jnp.sum(q_h.astype(F32) * n_h, axis=-1, keepdims=True)
        den = jnp.sum(s, axis=-1, keepdims=True) + w_inter * qn
        den = jnp.maximum(jnp.abs(den), jnp.exp(-m_t))
        hid = num / den
        y = _rms(hid) * gm_ref[h:h + 1, :] * og_ref[0, :, h * dv:(h + 1) * dv].astype(F32)
        y_ref[0, :, h * dv:(h + 1) * dv] = y.astype(y_ref.dtype)

        m_last = m_t[L - 1:L, :]
        b_last = b_col[L - 1:L, :]
        w_state = jnp.exp(b_last + m_prev - m_last)
        w_rows = jnp.exp(b_last - b_col + ig_col - m_last)
        kw = k_h.astype(F32) * w_rows
        c_ref[0, h] = w_state * c_h + _dot_tn(v_h, kw.astype(BF16))
        n_ref[0, h:h + 1, :] = w_state * n_h + jnp.sum(kw, axis=0, keepdims=True)
        m_ref[0, h:h + 1, :] = m_last


def _mlstm_t_kernel(q_ref, k_ref, vt_ref, og_ref, gt_ref, gmt_ref, y_ref, c_ref, n_ref, m_ref, st_ref,
                    *, rows, L, hm, dk, dv):
    ci = pl.program_id(1)

    @pl.when(ci == 0)
    def _():
        st_ref[...] = jnp.zeros_like(st_ref)
        m_ref[...] = jnp.zeros_like(m_ref)

    src = lax.broadcasted_iota(jnp.int32, (L, L), 0)
    tgt = lax.broadcasted_iota(jnp.int32, (L, L), 1)
    visible = src <= tgt
    lower = (tgt <= src).astype(BF16)
    ones_rows = (lax.broadcasted_iota(jnp.int32, (BF16_ROWS, L), 0) == 0).astype(BF16)

    gates_all = jnp.concatenate([gt_ref[r] for r in range(rows)], axis=-1)
    g1 = gates_all.astype(BF16)
    r1 = gates_all - g1.astype(F32)
    g2 = r1.astype(BF16)
    g3 = (r1 - g2.astype(F32)).astype(BF16)
    csum_all = _dot(lower, g1) + _dot(lower, g2) + _dot(lower, g3)

    row_vals = {}

    def decay_stage(r, h):
        if r not in row_vals:
            csum = csum_all[:, r * LANES:(r + 1) * LANES]
            row_vals[r] = (gt_ref[r], csum, csum.T)
        gates, csum, csum_t = row_vals[r]
        r_col = gates[:, h:h + 1] - csum[:, hm + h:hm + h + 1]
        r_lanes = jnp.broadcast_to(r_col, (L, LANES))
        b_row = csum_t[hm + h:hm + h + 1, :]
        m_prev = m_ref[r, h:h + 1, :]
        dmat = jnp.where(visible, jnp.concatenate([r_lanes] * (L // LANES), axis=-1) + b_row, -jnp.inf)
        inter = b_row + m_prev
        m_t = jnp.maximum(inter, jnp.max(dmat, axis=0, keepdims=True))
        m_last = m_t[:, L - 1:L]
        b_last = b_row[:, L - 1:L]
        return dict(
            w_intra=jnp.exp(dmat - m_t), w_inter=jnp.exp(inter - m_t), floor=jnp.exp(-m_t),
            w_state=jnp.exp(b_last + m_prev - m_last), w_rows=jnp.exp(r_lanes[:, :dk] + (b_last - m_last)),
            m_last=m_last)

    def matmul_stage(r, h, w):
        state = st_ref[r, h]
        q_h = q_ref[r, :, h * dk:(h + 1) * dk]
        k_h = k_ref[r, :, h * dk:(h + 1) * dk]
        vt_h = vt_ref[r, h * dv:(h + 1) * dv, :]
        s = _dot_nt(k_h, q_h) * w["w_intra"]
        sq = _dot_nt(state.astype(BF16), q_h)
        num = _dot(vt_h, s.astype(BF16)) + sq[:dv, :] * w["w_inter"]
        den = jnp.sum(s, axis=0, keepdims=True) + w["w_inter"] * sq[dv:dv + 1, :]
        hid = num / jnp.maximum(jnp.abs(den), w["floor"])
        scale = lax.rsqrt(jnp.mean(hid * hid, axis=0, keepdims=True) + NORM_EPS)
        gain = jnp.concatenate([gmt_ref[:, h * LANES:(h + 1) * LANES]] * (L // LANES), axis=-1)
        y_t = hid * scale * gain * og_ref[r, h * dv:(h + 1) * dv, :].astype(F32)
        y_ref[r, h * dv:(h + 1) * dv, :] = y_t.astype(y_ref.dtype)

        kw = (k_h.astype(F32) * w["w_rows"]).astype(BF16)
        vt_aug = jnp.concatenate([vt_h, ones_rows], axis=0)
        st_ref[r, h] = w["w_state"] * state + _dot(vt_aug, kw)
        m_ref[r, h:h + 1, :] = w["m_last"]

    chains = [(r, h) for r in range(rows) for h in range(hm)]
    pending = None
    for chain in chains + [None]:
        weights = decay_stage(*chain) if chain is not None else None
        if pending is not None:
            matmul_stage(*pending)
        pending = chain + (weights,) if chain is not None else None

    @pl.when(ci == pl.num_programs(1) - 1)
    def _():
        for r in range(rows):
            for h in range(hm):
                c_ref[r, h] = st_ref[r, h, :dv, :]
                n_ref[r, h:h + 1, :] = st_ref[r, h, dv:dv + 1, :]


def _mlstm(mq, mk, mv, og, gates, g_mlstm, *, L, rows=1, init=None):
    b, s, _ = mq.shape
    hm, dv = g_mlstm.shape
    dk = mq.shape[-1] // hm
    assert init is None or rows == 1
    tok = lambda width: pl.BlockSpec((rows, L, width), lambda i, j: (i, j, 0))
    st_c = pl.BlockSpec((rows, hm, dv, dk), lambda i, j: (i, 0, 0, 0))
    st_n = pl.BlockSpec((rows, hm, dk), lambda i, j: (i, 0, 0))
    st_m = pl.BlockSpec((rows, hm, 1), lambda i, j: (i, 0, 0))
    out_specs = [tok(hm * dv), st_c, st_n, st_m]
    out_shape = [jax.ShapeDtypeStruct((b, s, hm * dv), BF16),
                 jax.ShapeDtypeStruct((b, hm, dv, dk), F32),
                 jax.ShapeDtypeStruct((b, hm, dk), F32),
                 jax.ShapeDtypeStruct((b, hm, 1), F32)]
    if init is not None:
        return pl.pallas_call(
            functools.partial(_mlstm_kernel, L=L, hm=hm, dk=dk, dv=dv),
            grid=(b, s // L),
            in_specs=[tok(hm * dk), tok(hm * dk), tok(hm * dv), tok(hm * dv), tok(LANES),
                      _resident(g_mlstm.shape), st_c, st_n, st_m],
            out_specs=out_specs,
            out_shape=out_shape,
            compiler_params=_params("parallel", "arbitrary"),
            name="mlstm_init",
        )(mq, mk, mv, og, gates, g_mlstm, *init)
    assert L % LANES == 0
    g_t = jnp.broadcast_to(g_mlstm.T[:, :, None], (dv, hm, LANES)).reshape(dv, hm * LANES)
    feat = pl.BlockSpec((rows, hm * dv, L), lambda i, j: (i, 0, j))
    out_specs[0] = feat
    out_shape[0] = jax.ShapeDtypeStruct((b, hm * dv, s), BF16)
    return pl.pallas_call(
        functools.partial(_mlstm_t_kernel, rows=rows, L=L, hm=hm, dk=dk, dv=dv),
        grid=(b // rows, s // L),
        in_specs=[tok(hm * dk), tok(hm * dk), feat, feat, tok(LANES), _resident(g_t.shape)],
        out_specs=out_specs,
        out_shape=out_shape,
        scratch_shapes=[pltpu.VMEM((rows, hm, dv + BF16_ROWS, dk), F32)],
        compiler_params=_params("parallel", "arbitrary"),
        name="mlstm",
    )(mq, mk, mv, og, gates, g_t)


def _diff_lambda(lam_ref, layer):
    lq = lam_ref[...]
    a = jnp.sum(lq[0:1, :] * lq[1:2, :], axis=-1, keepdims=True)
    b = jnp.sum(lq[2:3, :] * lq[3:4, :], axis=-1, keepdims=True)
    lam_init = 0.8 - 0.6 * float(np.exp(-0.3 * layer))
    return jnp.exp(a) - jnp.exp(b) + lam_init, lam_init


def _split_maps(q):
    lane = lax.broadcasted_iota(jnp.int32, q.shape, 1)
    zero = jnp.zeros_like(q)
    half = q.shape[-1] // 2
    return jnp.where(lane < half, q, zero), jnp.where(lane >= half, q, zero)


def _attn_prompt_kernel(q_ref, k_ref, v_ref, lam_ref, g_ref, o_ref, vt_ref, s_ref, p_ref, *, tq, layer):
    s_len = k_ref.shape[1]
    n_heads, dv = g_ref.shape[0], g_ref.shape[-1]
    n_blk = s_len // tq
    for hh in range(n_heads):
        vt_ref[hh, dv:, :] = (lax.broadcasted_iota(jnp.int32, (BF16_ROWS, s_len), 0) == 0).astype(BF16)

    def lanes(hh):
        return slice(hh * dv, (hh + 1) * dv)

    def transpose_values(hh, j):
        v_blk = v_ref[0, j * tq:(j + 1) * tq, lanes(hh)]
        vt_ref[hh, :dv, j * tq:(j + 1) * tq] = v_blk.astype(F32).T.astype(BF16)

    lam, lam_init = _diff_lambda(lam_ref, layer)
    key_chunk = lax.broadcasted_iota(jnp.int32, (tq, tq), 0) // CHUNK
    qry_chunk = lax.broadcasted_iota(jnp.int32, (tq, tq), 1) // CHUNK
    visible = key_chunk <= qry_chunk

    def score_block(slot, hh, q_map, qi, j, m8):
        s = _dot_nt(k_ref[0, j * tq:(j + 1) * tq, lanes(hh)], q_map)
        if j == qi:
            s = jnp.where(visible, s, -jnp.inf)
        s_ref[slot, j] = s
        m_j = jnp.max(s.reshape(tq // 8, 8, tq), axis=0)
        return m_j if m8 is None else jnp.maximum(m8, m_j)

    def weight_block(slot, j, m):
        p_ref[slot, j * tq:(j + 1) * tq, :] = jnp.exp2(s_ref[slot, j] - m).astype(BF16)

    stages = [(hh, qi, c) for hh in range(n_heads) for qi in reversed(range(n_blk)) for c in range(2)]
    prev = None
    outs = {}
    for k, stage in enumerate(stages + [None]):
        n_score = 0
        if stage is not None:
            hh, qi, c = stage
            if c == 0:
                q_maps = _split_maps(q_ref[0, qi * tq:(qi + 1) * tq, lanes(hh)])
            n_score = qi + 1
        n_weight = prev[2] + 1 if prev is not None else 0
        m8 = None
        for j in range(max(n_score, n_weight)):
            if j < n_score:
                m8 = score_block(k % 2, hh, q_maps[c], qi, j, m8)
                if qi == n_blk - 1 and c == 0:
                    transpose_values(hh, j)
            if j < n_weight:
                weight_block(prev[0] % 2, j, prev[4])
        if prev is not None:
            pk, ph, pqi, pc, _ = prev
            n_keys = (pqi + 1) * tq
            acc = _dot(vt_ref[ph, :, :n_keys], p_ref[pk % 2, :n_keys, :])
            outs[pc] = acc[:dv, :] / acc[dv:dv + 1, :]
            if pc == 1:
                o = (outs[0] - lam * outs[1]).T
                o_ref[0, pqi * tq:(pqi + 1) * tq, lanes(ph)] = (
                    _rms(o) * g_ref[ph] * (1.0 - lam_init)).astype(o_ref.dtype)
        if stage is not None:
            prev = (k, hh, qi, c, jnp.max(m8, axis=0, keepdims=True))


def _attn_prompt(dq, dk, dv, lambda_qk, g_diff, *, tq, heads, layer):
    b, s, _ = dq.shape
    hd, dvh = g_diff.shape
    n_blk = s // tq
    head = pl.BlockSpec((1, s, heads * dvh), lambda i, h: (i, 0, h))
    return pl.pallas_call(
        functools.partial(_attn_prompt_kernel, tq=tq, layer=layer),
        grid=(b, hd // heads),
        in_specs=[head, head, head, _resident(lambda_qk.shape),
                  pl.BlockSpec((heads, 1, dvh), lambda i, h: (h, 0, 0))],
        out_specs=head,
        out_shape=jax.ShapeDtypeStruct((b, s, hd * dvh), BF16),
        scratch_shapes=[pltpu.VMEM((heads, dvh + BF16_ROWS, s), BF16),
                        pltpu.VMEM((2, n_blk, tq, tq), F32), pltpu.VMEM((2, s, tq), BF16)],
        compiler_params=_params("parallel", "parallel"),
        name="diff_attn_prompt",
    )(dq, dk, dv, lambda_qk, g_diff.reshape(hd, 1, dvh))


def _attn_sample_kernel(q_ref, kc_ref, vc_ref, kn_ref, vn_ref, lam_ref, g_ref, o_ref, *, hd, layer):
    lam, lam_init = _diff_lambda(lam_ref, layer)
    past = kc_ref.shape[1] // hd
    dvh = g_ref.shape[-1]
    for h in range(hd):
        cols = slice(h * dvh, (h + 1) * dvh)
        q_maps = _split_maps(q_ref[0, :, cols])
        k_c = kc_ref[0, pl.ds(h, past, stride=hd), :].astype(BF16)
        v_c = vc_ref[0, pl.ds(h, past, stride=hd), :].astype(BF16)
        k_n = kn_ref[0, :, cols]
        v_n = vn_ref[0, :, cols]
        outs = []
        for c in range(2):
            s_c = _dot_nt(q_maps[c], k_c)
            s_n = _dot_nt(q_maps[c], k_n)
            m = jnp.maximum(jnp.max(s_c, axis=-1, keepdims=True), jnp.max(s_n, axis=-1, keepdims=True))
            p_c = jnp.exp2(s_c - m)
            p_n = jnp.exp2(s_n - m)
            l = jnp.sum(p_c, axis=-1, keepdims=True) + jnp.sum(p_n, axis=-1, keepdims=True)
            outs.append((_dot(p_c.astype(BF16), v_c) + _dot(p_n.astype(BF16), v_n)) / l)
        o = outs[0] - lam * outs[1]
        o_ref[0, :, cols] = (_rms(o) * g_ref[h:h + 1, :] * (1.0 - lam_init)).astype(o_ref.dtype)


def _attn_sample(dq, dk, dv, cache_k, cache_v, lambda_qk, g_diff, *, layer):
    b, t, width = dq.shape
    hd, dvh = g_diff.shape
    new = pl.BlockSpec((1, t, width), lambda i: (i, 0, 0))
    old = pl.BlockSpec((1, cache_k.shape[1], dvh), lambda i: (i, 0, 0))
    return pl.pallas_call(
        functools.partial(_attn_sample_kernel, hd=hd, layer=layer),
        grid=(b,),
        in_specs=[new, old, old, new, new, _resident(lambda_qk.shape), _resident(g_diff.shape)],
        out_specs=new,
        out_shape=jax.ShapeDtypeStruct((b, t, width), BF16),
        compiler_params=_params("parallel"),
        name="diff_attn_sample",
    )(dq, cache_k, cache_v, dk, dv, lambda_qk, g_diff)


def _interleave_kernel(g_ref, u_ref, o_ref):
    for i in range(g_ref.shape[-1] // LANES):
        o_ref[:, 2 * i * LANES:(2 * i + 1) * LANES] = g_ref[0, :, i * LANES:(i + 1) * LANES].astype(BF16)
        o_ref[:, (2 * i + 1) * LANES:(2 * i + 2) * LANES] = u_ref[0, :, i * LANES:(i + 1) * LANES].astype(BF16)


def _interleave_gate_up(w_in_all, layer):
    _, d, two_f = w_in_all.shape
    f = two_f // 2
    n_groups = f // LANES
    per_step = 2 if n_groups % 2 == 0 else 1
    tw = per_step * LANES
    n_steps = n_groups // per_step
    return pl.pallas_call(
        _interleave_kernel,
        grid=(n_steps,),
        in_specs=[pl.BlockSpec((1, d, tw), lambda j: (layer, 0, j)),
                  pl.BlockSpec((1, d, tw), lambda j: (layer, 0, n_steps + j))],
        out_specs=pl.BlockSpec((d, 2 * tw), lambda j: (0, j)),
        out_shape=jax.ShapeDtypeStruct((d, two_f), BF16),
        compiler_params=_params("parallel"),
        name="interleave_gate_up",
    )(w_in_all, w_in_all)


def _rope_tables(pos, dqk):
    rope_dim = dqk // 4
    half = rope_dim // 2
    inv_freq = ROPE_THETA ** (-jnp.arange(half, dtype=F32) * (2.0 / rope_dim))
    ang = pos.astype(F32)[:, None] * inv_freq[None, :]
    cos, sin = jnp.cos(ang), jnp.sin(ang)
    n = pos.shape[0]
    pad = jnp.zeros((n, dqk - rope_dim), F32)
    zero = jnp.zeros((n, half), F32)
    cos_t = jnp.concatenate([cos, cos, pad + 1.0], axis=1)
    sa_t = jnp.concatenate([-sin, zero, pad], axis=1)
    sb_t = jnp.concatenate([zero, sin, pad], axis=1)
    rep = LANES // dqk
    return tuple(jnp.tile(a, (1, rep)) for a in (cos_t, sa_t, sb_t))


def _layer(x, c_mod, pos, attend, mlstm_init, weights, dims, *, nb, t, n_sub, L, rows, layer):
    (g_norm, w1_in, w1_out, w_proj, w_vo, bias, g_mlstm, g_diff, lambda_qk, w_o, w2_in, w2_out) = weights
    hm, dk_m, widths = dims
    v_transposed = mlstm_init is None
    x1 = _ffn_block(x, c_mod, g_norm, w1_in, w1_out, nb=nb, t=t, n_sub=n_sub, g_idx=(0, 1), m_idx=(0, 1, 2))
    dqk = g_diff.shape[1] // 2
    mq, mk, mv, og, dq, dk, dv, k_state, v_state, gates = _project(
        x1, c_mod, g_norm, w_proj, w_vo.T if v_transposed else w_vo, bias, _rope_tables(pos, dqk),
        nb=nb, t=t, n_sub=n_sub, hm=hm, widths=widths, k_scale=dk_m ** -0.5, q_scale=dqk ** -0.5 * LOG2_E,
        v_transposed=v_transposed)
    y_m, c_new, n_new, m_new = _mlstm(mq, mk, mv, og, gates, g_mlstm, L=L, rows=rows, init=mlstm_init)
    y_d = attend(dq, dk, dv)
    y = _ffn_block(x1, c_mod, g_norm, w2_in, w2_out, nb=nb, t=t, n_sub=n_sub, g_idx=(4, 5), m_idx=(6, 7, 8),
                   heads=(y_m, y_d), w_o=w_o, ym_transposed=v_transposed)
    return y, (k_state, v_state, c_new, n_new, m_new)


def kernel(x_prompt, x_sample, cache_k, cache_v, state_C, state_n, state_m, c_prompt, c_sample,
           w_ada, b_ada, g_norm, w_ffn1_in, w_ffn1_out, w_in, b_igate, b_fgate, g_mlstm, g_diff,
           lambda_qk, w_out, w_ffn2_in, w_ffn2_out):
    depth = w_ada.shape[0]
    bp, sp, d = x_prompt.shape
    bs, ts, _ = x_sample.shape
    past = cache_k.shape[2]
    hm, mv_dim = g_mlstm.shape[1:]
    hd, dv_dim = g_diff.shape[1:]
    mqk = mv_dim // 2
    dqk = dv_dim // 2
    sizes = [hm * mqk, hm * mqk, hm * mv_dim, hm * mv_dim, hm, hm, hd * 2 * dqk, hd * 2 * dqk, hd * dv_dim]
    offs = np.concatenate([[0], np.cumsum(sizes)])
    widths = tuple(sizes[i] for i in (0, 1, 2, 3, 6, 7, 8))
    t_prompt = min(512, sp)
    sub_prompt = 2 if sp % (2 * t_prompt) == 0 else 1
    l_prompt = min(256, sp)
    rows_prompt = max(r for r in (4, 2, 1) if bp % r == 0)

    y_p, y_s = x_prompt, x_sample
    states_p, states_s = [], []
    for l in range(depth):
        mod = _ada(jnp.concatenate([c_prompt, c_sample], axis=0), w_ada[l], b_ada[l])
        mod = mod.reshape(bp + bs, N_ADA, d)
        cols = [w_in[l][:, offs[i]:offs[i + 1]] for i in range(9)]
        gate_cols = jnp.concatenate([cols[4], cols[5], jnp.zeros((d, LANES - 2 * hm), F32)], axis=1)
        w_proj = jnp.concatenate([cols[i] for i in (0, 1, 6, 7, 8)] + [gate_cols], axis=1).astype(BF16)
        w_vo = jnp.concatenate([cols[2], cols[3]], axis=1).astype(BF16)
        bias = jnp.concatenate([b_igate[l], b_fgate[l], jnp.zeros((LANES - 2 * hm,), F32)]).reshape(1, LANES)
        weights = (g_norm[l], _interleave_gate_up(w_ffn1_in, l), w_ffn1_out[l].astype(BF16), w_proj,
                   w_vo, bias, g_mlstm[l], g_diff[l], lambda_qk[l], w_out[l].astype(BF16),
                   _interleave_gate_up(w_ffn2_in, l), w_ffn2_out[l].astype(BF16))
        dims = (hm, mqk, widths)

        attend_p = functools.partial(_attn_prompt, lambda_qk=lambda_qk[l], g_diff=g_diff[l],
                                     tq=min(256, sp), heads=2 if hd % 2 == 0 else 1, layer=l)
        y_p, st_p = _layer(y_p, mod[:bp], jnp.arange(sp), attend_p, None, weights, dims,
                           nb=1, t=t_prompt, n_sub=sub_prompt, L=l_prompt, rows=rows_prompt, layer=l)

        ck = cache_k[l].reshape(bs, past * hd, 2 * dqk)
        cv = cache_v[l].reshape(bs, past * hd, dv_dim)
        attend_s = functools.partial(_attn_sample, cache_k=ck, cache_v=cv, lambda_qk=lambda_qk[l],
                                     g_diff=g_diff[l], layer=l)
        init_s = (state_C[l], state_n[l], state_m[l].reshape(bs, hm, 1))
        y_s, st_s = _layer(y_s, mod[bp:], past + jnp.arange(ts), attend_s, init_s, weights, dims,
                           nb=bs, t=ts, n_sub=1, L=ts, rows=1, layer=l)
        states_p.append(st_p)
        states_s.append(st_s)

    def stack(states, b, s):
        if depth == 1:
            k, v, c, n, m = [a[None] for a in states[0]]
        else:
            k, v, c, n, m = [jnp.stack(a, axis=0) for a in zip(*states)]
        return (k.reshape(depth, b, s, hd, 2 * dqk), v.reshape(depth, b, s, hd, dv_dim),
                c, n, m.reshape(depth, b, hm))

    return (y_p, y_s) + stack(states_p, bp, sp) + stack(states_s, bs, ts)
```

```python
import functools

import jax
import jax.numpy as jnp
import numpy as np
from jax import lax
from jax.experimental import pallas as pl
from jax.experimental.pallas import tpu as pltpu

F32 = jnp.float32
BF16 = jnp.bfloat16

LANES = 128
BF16_ROWS = 16
MXU_N = 256
VMEM_LIMIT_BYTES = 56 << 20
NORM_EPS = 1e-6
ROPE_THETA = 500000.0
N_ADA = 9
CHUNK = 64
LOG2_E = 1.4426950408889634

_NT = (((1,), (1,)), ((), ()))
_TN = (((0,), (0,)), ((), ()))


def _dot(a, b):
    return jnp.dot(a, b, preferred_element_type=F32)


def _dot_nt(a, b):
    return lax.dot_general(a, b, _NT, preferred_element_type=F32)


def _dot_tn(a, b):
    return lax.dot_general(a, b, _TN, preferred_element_type=F32)


def _rms(x):
    return x * lax.rsqrt(jnp.mean(x * x, axis=-1, keepdims=True) + NORM_EPS)


def _silu(x):
    return x * jax.nn.sigmoid(x)


def _params(*sem):
    return pltpu.CompilerParams(dimension_semantics=sem, vmem_limit_bytes=VMEM_LIMIT_BYTES)


def _resident(shape):
    nd = len(shape)
    return pl.BlockSpec(shape, lambda *_: (0,) * nd, pipeline_mode=pl.Buffered(1))


def _ada_kernel(c_ref, w_ref, b_ref, o_ref):
    a = _silu(c_ref[...]).astype(BF16)
    o_ref[...] = _dot(a, w_ref[...].astype(BF16)) + b_ref[...]


def _ada(c, w_ada, b_ada):
    n, d = c.shape
    width = w_ada.shape[1]
    tn = width // 8
    return pl.pallas_call(
        _ada_kernel,
        grid=(width // tn,),
        in_specs=[pl.BlockSpec((n, d), lambda j: (0, 0)),
                  pl.BlockSpec((d, tn), lambda j: (0, j)),
                  pl.BlockSpec((1, tn), lambda j: (0, j))],
        out_specs=pl.BlockSpec((n, tn), lambda j: (0, j)),
        out_shape=jax.ShapeDtypeStruct((n, width), F32),
        compiler_params=_params("parallel"),
        name="ada",
    )(c, w_ada, b_ada.reshape(1, width))


def _ffn_kernel(*refs, nb, t, n_sub, pre_proj, ym_transposed, g_idx, m_idx):
    if pre_proj:
        x_ref, ym_ref, yd_ref, mod_ref, g_ref, wo_ref, w_in_ref, w_out_ref, o_ref, h_ref, act_ref = refs
    else:
        x_ref, mod_ref, g_ref, w_in_ref, w_out_ref, o_ref, h_ref, act_ref = refs
    d = x_ref.shape[-1]
    ts = t // n_sub
    rows = nb * ts
    g_a, g_b = g_idx
    i_shift, i_scale, i_gate = m_idx
    n_groups = w_out_ref.shape[0] // LANES

    def mrow(i):
        return mod_ref[:, i:i + 1, :]

    def grow(i):
        return g_ref[i:i + 1, :][None]

    def head_proj(i):
        tok = slice(i * ts, (i + 1) * ts)
        if ym_transposed:
            w_m = ym_ref.shape[1]
            return _dot_tn(ym_ref[0, :, tok], wo_ref[:w_m, :]) + _dot(yd_ref[0, tok, :], wo_ref[w_m:, :])
        heads = jnp.concatenate([ym_ref[:, tok, :], yd_ref[:, tok, :]], axis=-1).reshape(rows, d)
        return _dot(heads, wo_ref[...])

    def prologue(i, o=None):
        tok = slice(i * ts, (i + 1) * ts)
        x = x_ref[:, tok, :]
        if pre_proj:
            o = head_proj(i) if o is None else o
            x = x + mrow(5) * (_rms(o.reshape(nb, ts, d)) * grow(3))
        h = _rms(x) * grow(g_a) * (1.0 + mrow(i_scale)) + mrow(i_shift)
        h_ref[i * rows:(i + 1) * rows, :] = h.astype(BF16).reshape(rows, d)
        o_ref[:, tok, :] = x

    def chunk(i, j):
        gu = _dot(h_ref[i * rows:(i + 1) * rows, :], w_in_ref[:, j * MXU_N:(j + 1) * MXU_N])
        act = _silu(gu[:, :LANES]) * gu[:, LANES:]
        act_ref[i * rows:(i + 1) * rows, j * LANES:(j + 1) * LANES] = act.astype(BF16)

    def epilogue(tok, out):
        n_tok = tok.stop - tok.start
        o_ref[:, tok, :] = o_ref[:, tok, :] + (0.5 * mrow(i_gate)) * (_rms(out.reshape(nb, n_tok, d)) * grow(g_b))

    early = {i: head_proj(i) for i in range(min(n_sub, 2))} if pre_proj else {}
    prologue(0, early.get(0))
    waiting = None
    for i in range(n_sub):
        for j in range(n_groups):
            chunk(i, j)
            if j == 1:
                if i + 1 < n_sub:
                    prologue(i + 1, early.get(i + 1))
                if waiting is not None:
                    epilogue(*waiting)
                    waiting = None
        parts = 2 if (i == n_sub - 1 and nb == 1 and ts % (2 * BF16_ROWS) == 0) else 1
        step = ts // parts
        for p in range(parts):
            r0 = i * rows + p * step
            out = _dot(act_ref[r0:r0 + step * nb, :], w_out_ref[...])
            if waiting is not None:
                epilogue(*waiting)
            waiting = (slice(i * ts + p * step, i * ts + (p + 1) * step), out)
    epilogue(*waiting)


def _ffn_block(x, mod, g_norm, w_in_r, w_out, *, nb, t, n_sub, g_idx, m_idx, heads=None, w_o=None,
               ym_transposed=False):
    b, s, d = x.shape
    t = t * n_sub
    f = w_out.shape[0]
    pre_proj = heads is not None
    grid = (b // nb, s // t)
    tok = lambda width: pl.BlockSpec((nb, t, width), lambda i, j: (i, j, 0))
    in_specs = [tok(d)]
    args = [x]
    if pre_proj:
        ym, yd = heads
        if ym_transposed:
            assert nb == 1
            ym_spec = pl.BlockSpec((1, ym.shape[1], t), lambda i, j: (i, 0, j))
        else:
            ym_spec = tok(ym.shape[-1])
        in_specs += [ym_spec, tok(yd.shape[-1])]
        args += [ym, yd]
    in_specs += [pl.BlockSpec((nb, N_ADA, d), lambda i, j: (i, 0, 0)), _resident(g_norm.shape)]
    args += [mod, g_norm]
    if pre_proj:
        in_specs.append(_resident(w_o.shape))
        args.append(w_o)
    in_specs += [_resident(w_in_r.shape), _resident(w_out.shape)]
    args += [w_in_r, w_out]
    return pl.pallas_call(
        functools.partial(_ffn_kernel, nb=nb, t=t, n_sub=n_sub, pre_proj=pre_proj, ym_transposed=ym_transposed,
                          g_idx=g_idx, m_idx=m_idx),
        grid=grid,
        in_specs=in_specs,
        out_specs=tok(d),
        out_shape=jax.ShapeDtypeStruct((b, s, d), F32),
        scratch_shapes=[pltpu.VMEM((nb * t, d), BF16), pltpu.VMEM((nb * t, f), BF16)],
        compiler_params=_params("parallel", "parallel"),
        name="ffn_post" if pre_proj else "ffn_pre",
    )(*args)


def _proj_kernel(x_ref, mod_ref, g_ref, w_ref, wv_ref, bias_ref, cos_ref, sa_ref, sb_ref,
                 mq_ref, mk_ref, mv_ref, og_ref, dq_ref, dk_ref, dv_ref, kst_ref, vst_ref, gt_ref, h_ref,
                 *, nb, t, n_sub, hm, widths, k_scale, q_scale, v_transposed):
    d = x_ref.shape[-1]
    ts = t // n_sub
    rows = nb * ts
    w_mq, w_mk, w_mv, w_mo, w_dq, w_dk, w_dv = widths
    n_heads = w_dk // LANES

    def prologue(i):
        tok = slice(i * ts, (i + 1) * ts)
        h = _rms(x_ref[:, tok, :]) * g_ref[2:3, :][None] * (1.0 + mod_ref[:, 4:5, :]) + mod_ref[:, 3:4, :]
        h_ref[i * rows:(i + 1) * rows, :] = h.astype(BF16).reshape(rows, d)

    def project(i):
        tok = slice(i * ts, (i + 1) * ts)
        h_rows = slice(i * rows, (i + 1) * rows)

        def mm(c0, width):
            return _dot(h_ref[h_rows, :], w_ref[:, c0:c0 + width])

        def put(ref, c, val):
            width = val.shape[-1]
            ref[:, tok, c:c + width] = val.astype(ref.dtype).reshape(nb, ts, width)

        def put_state(ref, c, val):
            ref[:, pl.ds(i * ts * n_heads + c // LANES, ts, stride=n_heads), :] = val.reshape(nb, ts, LANES)

        def rope(xb):
            x3 = xb.reshape(nb, ts, LANES)
            up = pltpu.roll(xb, LANES - 8, 1).reshape(nb, ts, LANES)
            dn = pltpu.roll(xb, 8, 1).reshape(nb, ts, LANES)
            r = x3 * cos_ref[tok, :][None] + up * sa_ref[tok, :][None] + dn * sb_ref[tok, :][None]
            return r.reshape(rows, LANES)

        c0 = 0
        for c in range(0, w_mq, MXU_N):
            put(mq_ref, c, mm(c0 + c, min(MXU_N, w_mq - c)))
        c0 += w_mq
        if i + 1 < n_sub:
            prologue(i + 1)
        for c in range(0, w_mk, MXU_N):
            put(mk_ref, c, mm(c0 + c, min(MXU_N, w_mk - c)) * k_scale)
        c0 += w_mk
        for c in range(0, w_mv, MXU_N):
            if v_transposed:
                mv_ref[0, c:c + MXU_N, tok] = _dot_nt(wv_ref[c:c + MXU_N, :], h_ref[h_rows, :]).astype(mv_ref.dtype)
            else:
                put(mv_ref, c, _dot(h_ref[h_rows, :], wv_ref[:, c:c + MXU_N]))
        for c in range(0, w_mo, MXU_N):
            if v_transposed:
                pre = _dot_nt(wv_ref[w_mv + c:w_mv + c + MXU_N, :], h_ref[h_rows, :])
                og_ref[0, c:c + MXU_N, tok] = jax.nn.sigmoid(pre).astype(og_ref.dtype)
            else:
                put(og_ref, c, jax.nn.sigmoid(_dot(h_ref[h_rows, :], wv_ref[:, w_mv + c:w_mv + c + MXU_N])))
        for c in range(0, w_dq, MXU_N):
            r = mm(c0 + c, MXU_N)
            for k in range(0, MXU_N, LANES):
                put(dq_ref, c + k, rope(r[:, k:k + LANES]) * q_scale)
        c0 += w_dq
        for c in range(0, w_dk, MXU_N):
            r = mm(c0 + c, MXU_N)
            for k in range(0, MXU_N, LANES):
                k_rot = rope(r[:, k:k + LANES])
                put(dk_ref, c + k, k_rot)
                put_state(kst_ref, c + k, k_rot)
        c0 += w_dk
        for c in range(0, w_dv, MXU_N):
            r = mm(c0 + c, MXU_N)
            put(dv_ref, c, r)
            for k in range(0, MXU_N, LANES):
                put_state(vst_ref, c + k, r[:, k:k + LANES])
        c0 += w_dv
        z = mm(c0, LANES) + bias_ref[...]
        log_sig = jnp.minimum(z, 0.0) - jnp.log1p(jnp.exp(-jnp.abs(z)))
        lane = lax.broadcasted_iota(jnp.int32, z.shape, 1)
        put(gt_ref, 0, jnp.where(lane < hm, z, jnp.where(lane < 2 * hm, log_sig, 0.0)))

    prologue(0)
    for i in range(n_sub):
        project(i)


def _project(x, mod, g_norm, w_proj, w_mv, bias, tables, *, nb, t, n_sub, hm, widths, k_scale, q_scale,
             v_transposed):
    b, s, d = x.shape
    t = t * n_sub
    cos_t, sa_t, sb_t = tables
    grid = (b // nb, s // t)
    tok = lambda width: pl.BlockSpec((nb, t, width), lambda i, j: (i, j, 0))
    tab = pl.BlockSpec((t, LANES), lambda i, j: (j, 0))
    w_mq, w_mk, w_mv_, w_mo, w_dq, w_dk, w_dv = widths
    n_heads = w_dk // LANES
    out_widths = [(w_mq, BF16), (w_mk, BF16), (w_mv_, BF16), (w_mo, BF16),
                  (w_dq, BF16), (w_dk, BF16), (w_dv, BF16), (0, F32), (0, F32), (LANES, F32)]
    out_specs = [tok(w) for w, _ in out_widths]
    out_shape = [jax.ShapeDtypeStruct((b, s, w), dt) for w, dt in out_widths]
    for i in (7, 8):
        out_specs[i] = pl.BlockSpec((nb, t * n_heads, LANES), lambda i, j: (i, j, 0))
        out_shape[i] = jax.ShapeDtypeStruct((b, s * n_heads, LANES), F32)
    if v_transposed:
        assert nb == 1
        for i, w in ((2, w_mv_), (3, w_mo)):
            out_specs[i] = pl.BlockSpec((1, w, t), lambda i, j: (i, 0, j))
            out_shape[i] = jax.ShapeDtypeStruct((b, w, s), BF16)
    return pl.pallas_call(
        functools.partial(_proj_kernel, nb=nb, t=t, n_sub=n_sub, hm=hm, widths=widths,
                          k_scale=k_scale, q_scale=q_scale, v_transposed=v_transposed),
        grid=grid,
        in_specs=[tok(d), pl.BlockSpec((nb, N_ADA, d), lambda i, j: (i, 0, 0)),
                  _resident(g_norm.shape), _resident(w_proj.shape), _resident(w_mv.shape),
                  _resident(bias.shape), tab, tab, tab],
        out_specs=out_specs,
        out_shape=out_shape,
        scratch_shapes=[pltpu.VMEM((nb * t, d), BF16)],
        compiler_params=_params("parallel", "parallel"),
        name="in_proj",
    )(x, mod, g_norm, w_proj, w_mv, bias, cos_t, sa_t, sb_t)


def _mlstm_kernel(q_ref, k_ref, v_ref, og_ref, gt_ref, gm_ref, c0_ref, n0_ref, m0_ref,
                  y_ref, c_ref, n_ref, m_ref, *, L, hm, dk, dv):
    @pl.when(pl.program_id(1) == 0)
    def _():
        c_ref[...] = c0_ref[...]
        n_ref[...] = n0_ref[...]
        m_ref[...] = m0_ref[...]

    gates = gt_ref[0]
    row = lax.broadcasted_iota(jnp.int32, (L, L), 0)
    col = lax.broadcasted_iota(jnp.int32, (L, L), 1)
    causal = col <= row
    diag = col == row

    def as_row(col_vec):
        return jnp.sum(jnp.where(diag, col_vec, 0.0), axis=0, keepdims=True)

    for h in range(hm):
        ig_col = gates[:, h:h + 1]
        lf_row = as_row(gates[:, hm + h:hm + h + 1])
        b_col = jnp.sum(jnp.where(causal, lf_row, 0.0), axis=-1, keepdims=True)
        r_row = as_row(ig_col - b_col)
        m_prev = m_ref[0, h:h + 1, :]
        c_h = c_ref[0, h]
        n_h = n_ref[0, h:h + 1, :]
        q_h = q_ref[0, :, h * dk:(h + 1) * dk]
        k_h = k_ref[0, :, h * dk:(h + 1) * dk]
        v_h = v_ref[0, :, h * dv:(h + 1) * dv]

        dmat = jnp.where(causal, b_col + r_row, -jnp.inf)
        inter = b_col + m_prev
        m_t = jnp.maximum(inter, jnp.max(dmat, axis=-1, keepdims=True))
        w_inter = jnp.exp(inter - m_t)
        s = _dot_nt(q_h, k_h) * jnp.exp(dmat - m_t)
        num = _dot(s.astype(BF16), v_h) + _dot_nt(q_h, c_h.astype(BF16)) * w_inter
        qn = jnp.sum(q_h.astype(F32) * n_h, axis=-1, keepdims=True)
        den = jnp.sum(s, axis=-1, keepdims=True) + w_inter * qn
        den = jnp.maximum(jnp.abs(den), jnp.exp(-m_t))
        hid = num / den
        y = _rms(hid) * gm_ref[h:h + 1, :] * og_ref[0, :, h * dv:(h + 1) * dv].astype(F32)
        y_ref[0, :, h * dv:(h + 1) * dv] = y.astype(y_ref.dtype)

        m_last = m_t[L - 1:L, :]
        b_last = b_col[L - 1:L, :]
        w_state = jnp.exp(b_last + m_prev - m_last)
        w_rows = jnp.exp(b_last - b_col + ig_col - m_last)
        kw = k_h.astype(F32) * w_rows
        c_ref[0, h] = w_state * c_h + _dot_tn(v_h, kw.astype(BF16))
        n_ref[0, h:h + 1, :] = w_state * n_h + jnp.sum(kw, axis=0, keepdims=True)
        m_ref[0, h:h + 1, :] = m_last


def _mlstm_t_kernel(q_ref, k_ref, vt_ref, og_ref, gt_ref, gmt_ref, y_ref, c_ref, n_ref, m_ref, st_ref,
                    *, rows, L, hm, dk, dv):
    ci = pl.program_id(1)

    @pl.when(ci == 0)
    def _():
        st_ref[...] = jnp.zeros_like(st_ref)
        m_ref[...] = jnp.zeros_like(m_ref)

    src = lax.broadcasted_iota(jnp.int32, (L, L), 0)
    tgt = lax.broadcasted_iota(jnp.int32, (L, L), 1)
    visible = src <= tgt
    lower = (tgt <= src).astype(BF16)
    ones_rows = (lax.broadcasted_iota(jnp.int32, (BF16_ROWS, L), 0) == 0).astype(BF16)

    gates_all = jnp.concatenate([gt_ref[r] for r in range(rows)], axis=-1)
    g1 = gates_all.astype(BF16)
    r1 = gates_all - g1.astype(F32)
    g2 = r1.astype(BF16)
    g3 = (r1 - g2.astype(F32)).astype(BF16)
    csum_all = _dot(lower, g1) + _dot(lower, g2) + _dot(lower, g3)

    row_vals = {}

    def decay_stage(r, h):
        if r not in row_vals:
            csum = csum_all[:, r * LANES:(r + 1) * LANES]
            row_vals[r] = (gt_ref[r], csum, csum.T)
        gates, csum, csum_t = row_vals[r]
        r_col = gates[:, h:h + 1] - csum[:, hm + h:hm + h + 1]
        r_lanes = jnp.broadcast_to(r_col, (L, LANES))
        b_row = csum_t[hm + h:hm + h + 1, :]
        m_prev = m_ref[r, h:h + 1, :]
        dmat = jnp.where(visible, jnp.concatenate([r_lanes] * (L // LANES), axis=-1) + b_row, -jnp.inf)
        inter = b_row + m_prev
        m_t = jnp.maximum(inter, jnp.max(dmat, axis=0, keepdims=True))
        m_last = m_t[:, L - 1:L]
        b_last = b_row[:, L - 1:L]
        return dict(
            w_intra=jnp.exp(dmat - m_t), w_inter=jnp.exp(inter - m_t), floor=jnp.exp(-m_t),
            w_state=jnp.exp(b_last + m_prev - m_last), w_rows=jnp.exp(r_lanes[:, :dk] + (b_last - m_last)),
            m_last=m_last)

    def matmul_stage(r, h, w):
        state = st_ref[r, h]
        q_h = q_ref[r, :, h * dk:(h + 1) * dk]
        k_h = k_ref[r, :, h * dk:(h + 1) * dk]
        vt_h = vt_ref[r, h * dv:(h + 1) * dv, :]
        s = _dot_nt(k_h, q_h) * w["w_intra"]
        sq = _dot_nt(state.astype(BF16), q_h)
        num = _dot(vt_h, s.astype(BF16)) + sq[:dv, :] * w["w_inter"]
        den = jnp.sum(s, axis=0, keepdims=True) + w["w_inter"] * sq[dv:dv + 1, :]
        hid = num / jnp.maximum(jnp.abs(den), w["floor"])
        scale = lax.rsqrt(jnp.mean(hid * hid, axis=0, keepdims=True) + NORM_EPS)
        gain = jnp.concatenate([gmt_ref[:, h * LANES:(h + 1) * LANES]] * (L // LANES), axis=-1)
        y_t = hid * scale * gain * og_ref[r, h * dv:(h + 1) * dv, :].astype(F32)
        y_ref[r, h * dv:(h + 1) * dv, :] = y_t.astype(y_ref.dtype)

        kw = (k_h.astype(F32) * w["w_rows"]).astype(BF16)
        vt_aug = jnp.concatenate([vt_h, ones_rows], axis=0)
        st_ref[r, h] = w["w_state"] * state + _dot(vt_aug, kw)
        m_ref[r, h:h + 1, :] = w["m_last"]

    chains = [(r, h) for r in range(rows) for h in range(hm)]
    pending = None
    for chain in chains + [None]:
        weights = decay_stage(*chain) if chain is not None else None
        if pending is not None:
            matmul_stage(*pending)
        pending = chain + (weights,) if chain is not None else None

    @pl.when(ci == pl.num_programs(1) - 1)
    def _():
        for r in range(rows):
            for h in range(hm):
                c_ref[r, h] = st_ref[r, h, :dv, :]
                n_ref[r, h:h + 1, :] = st_ref[r, h, dv:dv + 1, :]


def _mlstm(mq, mk, mv, og, gates, g_mlstm, *, L, rows=1, init=None):
    b, s, _ = mq.shape
    hm, dv = g_mlstm.shape
    dk = mq.shape[-1] // hm
    assert init is None or rows == 1
    tok = lambda width: pl.BlockSpec((rows, L, width), lambda i, j: (i, j, 0))
    st_c = pl.BlockSpec((rows, hm, dv, dk), lambda i, j: (i, 0, 0, 0))
    st_n = pl.BlockSpec((rows, hm, dk), lambda i, j: (i, 0, 0))
    st_m = pl.BlockSpec((rows, hm, 1), lambda i, j: (i, 0, 0))
    out_specs = [tok(hm * dv), st_c, st_n, st_m]
    out_shape = [jax.ShapeDtypeStruct((b, s, hm * dv), BF16),
                 jax.ShapeDtypeStruct((b, hm, dv, dk), F32),
                 jax.ShapeDtypeStruct((b, hm, dk), F32),
                 jax.ShapeDtypeStruct((b, hm, 1), F32)]
    if init is not None:
        return pl.pallas_call(
            functools.partial(_mlstm_kernel, L=L, hm=hm, dk=dk, dv=dv),
            grid=(b, s // L),
            in_specs=[tok(hm * dk), tok(hm * dk), tok(hm * dv), tok(hm * dv), tok(LANES),
                      _resident(g_mlstm.shape), st_c, st_n, st_m],
            out_specs=out_specs,
            out_shape=out_shape,
            compiler_params=_params("parallel", "arbitrary"),
            name="mlstm_init",
        )(mq, mk, mv, og, gates, g_mlstm, *init)
    assert L % LANES == 0
    g_t = jnp.broadcast_to(g_mlstm.T[:, :, None], (dv, hm, LANES)).reshape(dv, hm * LANES)
    feat = pl.BlockSpec((rows, hm * dv, L), lambda i, j: (i, 0, j))
    out_specs[0] = feat
    out_shape[0] = jax.ShapeDtypeStruct((b, hm * dv, s), BF16)
    return pl.pallas_call(
        functools.partial(_mlstm_t_kernel, rows=rows, L=L, hm=hm, dk=dk, dv=dv),
        grid=(b // rows, s // L),
        in_specs=[tok(hm * dk), tok(hm * dk), feat, feat, tok(LANES), _resident(g_t.shape)],
        out_specs=out_specs,
        out_shape=out_shape,
        scratch_shapes=[pltpu.VMEM((rows, hm, dv + BF16_ROWS, dk), F32)],
        compiler_params=_params("parallel", "arbitrary"),
        name="mlstm",
    )(mq, mk, mv, og, gates, g_t)


def _diff_lambda(lam_ref, layer):
    lq = lam_ref[...]
    a = jnp.sum(lq[0:1, :] * lq[1:2, :], axis=-1, keepdims=True)
    b = jnp.sum(lq[2:3, :] * lq[3:4, :], axis=-1, keepdims=True)
    lam_init = 0.8 - 0.6 * float(np.exp(-0.3 * layer))
    return jnp.exp(a) - jnp.exp(b) + lam_init, lam_init


def _split_maps(q):
    lane = lax.broadcasted_iota(jnp.int32, q.shape, 1)
    zero = jnp.zeros_like(q)
    half = q.shape[-1] // 2
    return jnp.where(lane < half, q, zero), jnp.where(lane >= half, q, zero)


def _attn_prompt_kernel(q_ref, k_ref, v_ref, lam_ref, g_ref, o_ref, vt_ref, s_ref, p_ref, *, tq, layer):
    s_len = k_ref.shape[1]
    n_heads, dv = g_ref.shape[0], g_ref.shape[-1]
    n_blk = s_len // tq
    for hh in range(n_heads):
        vt_ref[hh, dv:, :] = (lax.broadcasted_iota(jnp.int32, (BF16_ROWS, s_len), 0) == 0).astype(BF16)

    def lanes(hh):
        return slice(hh * dv, (hh + 1) * dv)

    def transpose_values(hh, j):
        v_blk = v_ref[0, j * tq:(j + 1) * tq, lanes(hh)]
        vt_ref[hh, :dv, j * tq:(j + 1) * tq] = v_blk.astype(F32).T.astype(BF16)

    lam, lam_init = _diff_lambda(lam_ref, layer)
    key_chunk = lax.broadcasted_iota(jnp.int32, (tq, tq), 0) // CHUNK
    qry_chunk = lax.broadcasted_iota(jnp.int32, (tq, tq), 1) // CHUNK
    visible = key_chunk <= qry_chunk
    visible = jnp.concatenate([visible, visible], axis=1)

    def score_block(slot, hh, q_both, qi, j, m8):
        s = _dot_nt(k_ref[0, j * tq:(j + 1) * tq, lanes(hh)], q_both)
        if j == qi:
            s = jnp.where(visible, s, -jnp.inf)
        s_ref[slot, j] = s
        m_j = jnp.max(s.reshape(tq // 8, 8, 2 * tq), axis=0)
        return m_j if m8 is None else jnp.maximum(m8, m_j)

    def weight_block(slot, j, m):
        p_ref[slot, j * tq:(j + 1) * tq, :] = jnp.exp2(s_ref[slot, j] - m).astype(BF16)

    stages = [(hh, qi) for hh in range(n_heads) for qi in reversed(range(n_blk))]
    prev = None
    for k, stage in enumerate(stages + [None]):
        n_score = 0
        if stage is not None:
            hh, qi = stage
            q_both = jnp.concatenate(_split_maps(q_ref[0, qi * tq:(qi + 1) * tq, lanes(hh)]), axis=0)
            n_score = qi + 1
        n_weight = prev[2] + 1 if prev is not None else 0
        m8 = None
        for j in range(max(n_score, n_weight)):
            if j < n_score:
                m8 = score_block(k % 2, hh, q_both, qi, j, m8)
                if qi == n_blk - 1:
                    transpose_values(hh, j)
            if j < n_weight:
                weight_block(prev[0] % 2, j, prev[3])
        if prev is not None:
            pk, ph, pqi, _ = prev
            n_keys = (pqi + 1) * tq
            acc = _dot(vt_ref[ph, :, :n_keys], p_ref[pk % 2, :n_keys, :])
            out = acc[:dv, :] / acc[dv:dv + 1, :]
            o = (out[:, :tq] - lam * out[:, tq:]).T
            o_ref[0, pqi * tq:(pqi + 1) * tq, lanes(ph)] = (
                _rms(o) * g_ref[ph] * (1.0 - lam_init)).astype(o_ref.dtype)
        if stage is not None:
            prev = (k, hh, qi, jnp.max(m8, axis=0, keepdims=True))


def _attn_prompt(dq, dk, dv, lambda_qk, g_diff, *, tq, heads, layer):
    b, s, _ = dq.shape
    hd, dvh = g_diff.shape
    n_blk = s // tq
    head = pl.BlockSpec((1, s, heads * dvh), lambda i, h: (i, 0, h))
    return pl.pallas_call(
        functools.partial(_attn_prompt_kernel, tq=tq, layer=layer),
        grid=(b, hd // heads),
        in_specs=[head, head, head, _resident(lambda_qk.shape),
                  pl.BlockSpec((heads, 1, dvh), lambda i, h: (h, 0, 0))],
        out_specs=head,
        out_shape=jax.ShapeDtypeStruct((b, s, hd * dvh), BF16),
        scratch_shapes=[pltpu.VMEM((heads, dvh + BF16_ROWS, s), BF16),
                        pltpu.VMEM((2, n_blk, tq, 2 * tq), F32), pltpu.VMEM((2, s, 2 * tq), BF16)],
        compiler_params=_params("parallel", "parallel"),
        name="diff_attn_prompt",
    )(dq, dk, dv, lambda_qk, g_diff.reshape(hd, 1, dvh))


def _attn_sample_kernel(q_ref, kc_ref, vc_ref, kn_ref, vn_ref, lam_ref, g_ref, o_ref, *, hd, layer):
    lam, lam_init = _diff_lambda(lam_ref, layer)
    past = kc_ref.shape[1] // hd
    dvh = g_ref.shape[-1]
    for h in range(hd):
        cols = slice(h * dvh, (h + 1) * dvh)
        q_maps = _split_maps(q_ref[0, :, cols])
        k_c = kc_ref[0, pl.ds(h, past, stride=hd), :].astype(BF16)
        v_c = vc_ref[0, pl.ds(h, past, stride=hd), :].astype(BF16)
        k_n = kn_ref[0, :, cols]
        v_n = vn_ref[0, :, cols]
        outs = []
        for c in range(2):
            s_c = _dot_nt(q_maps[c], k_c)
            s_n = _dot_nt(q_maps[c], k_n)
            m = jnp.maximum(jnp.max(s_c, axis=-1, keepdims=True), jnp.max(s_n, axis=-1, keepdims=True))
            p_c = jnp.exp2(s_c - m)
            p_n = jnp.exp2(s_n - m)
            l = jnp.sum(p_c, axis=-1, keepdims=True) + jnp.sum(p_n, axis=-1, keepdims=True)
            outs.append((_dot(p_c.astype(BF16), v_c) + _dot(p_n.astype(BF16), v_n)) / l)
        o = outs[0] - lam * outs[1]
        o_ref[0, :, cols] = (_rms(o) * g_ref[h:h + 1, :] * (1.0 - lam_init)).astype(o_ref.dtype)


def _attn_sample(dq, dk, dv, cache_k, cache_v, lambda_qk, g_diff, *, layer):
    b, t, width = dq.shape
    hd, dvh = g_diff.shape
    new = pl.BlockSpec((1, t, width), lambda i: (i, 0, 0))
    old = pl.BlockSpec((1, cache_k.shape[1], dvh), lambda i: (i, 0, 0))
    return pl.pallas_call(
        functools.partial(_attn_sample_kernel, hd=hd, layer=layer),
        grid=(b,),
        in_specs=[new, old, old, new, new, _resident(lambda_qk.shape), _resident(g_diff.shape)],
        out_specs=new,
        out_shape=jax.ShapeDtypeStruct((b, t, width), BF16),
        compiler_params=_params("parallel"),
        name="diff_attn_sample",
    )(dq, cache_k, cache_v, dk, dv, lambda_qk, g_diff)


def _interleave_kernel(g_ref, u_ref, o_ref):
    for i in range(g_ref.shape[-1] // LANES):
        o_ref[:, 2 * i * LANES:(2 * i + 1) * LANES] = g_ref[0, :, i * LANES:(i + 1) * LANES].astype(BF16)
        o_ref[:, (2 * i + 1) * LANES:(2 * i + 2) * LANES] = u_ref[0, :, i * LANES:(i + 1) * LANES].astype(BF16)


def _interleave_gate_up(w_in_all, layer):
    _, d, two_f = w_in_all.shape
    f = two_f // 2
    n_groups = f // LANES
    per_step = 2 if n_groups % 2 == 0 else 1
    tw = per_step * LANES
    n_steps = n_groups // per_step
    return pl.pallas_call(
        _interleave_kernel,
        grid=(n_steps,),
        in_specs=[pl.BlockSpec((1, d, tw), lambda j: (layer, 0, j)),
                  pl.BlockSpec((1, d, tw), lambda j: (layer, 0, n_steps + j))],
        out_specs=pl.BlockSpec((d, 2 * tw), lambda j: (0, j)),
        out_shape=jax.ShapeDtypeStruct((d, two_f), BF16),
        compiler_params=_params("parallel"),
        name="interleave_gate_up",
    )(w_in_all, w_in_all)


def _rope_tables(pos, dqk):
    rope_dim = dqk // 4
    half = rope_dim // 2
    inv_freq = ROPE_THETA ** (-jnp.arange(half, dtype=F32) * (2.0 / rope_dim))
    ang = pos.astype(F32)[:, None] * inv_freq[None, :]
    cos, sin = jnp.cos(ang), jnp.sin(ang)
    n = pos.shape[0]
    pad = jnp.zeros((n, dqk - rope_dim), F32)
    zero = jnp.zeros((n, half), F32)
    cos_t = jnp.concatenate([cos, cos, pad + 1.0], axis=1)
    sa_t = jnp.concatenate([-sin, zero, pad], axis=1)
    sb_t = jnp.concatenate([zero, sin, pad], axis=1)
    rep = LANES // dqk
    return tuple(jnp.tile(a, (1, rep)) for a in (cos_t, sa_t, sb_t))


def _layer(x, c_mod, pos, attend, mlstm_init, weights, dims, *, nb, t, n_sub, L, rows, layer):
    (g_norm, w1_in, w1_out, w_proj, w_vo, bias, g_mlstm, g_diff, lambda_qk, w_o, w2_in, w2_out) = weights
    hm, dk_m, widths = dims
    v_transposed = mlstm_init is None
    x1 = _ffn_block(x, c_mod, g_norm, w1_in, w1_out, nb=nb, t=t, n_sub=n_sub, g_idx=(0, 1), m_idx=(0, 1, 2))
    dqk = g_diff.shape[1] // 2
    mq, mk, mv, og, dq, dk, dv, k_state, v_state, gates = _project(
        x1, c_mod, g_norm, w_proj, w_vo.T if v_transposed else w_vo, bias, _rope_tables(pos, dqk),
        nb=nb, t=t, n_sub=n_sub, hm=hm, widths=widths, k_scale=dk_m ** -0.5, q_scale=dqk ** -0.5 * LOG2_E,
        v_transposed=v_transposed)
    y_m, c_new, n_new, m_new = _mlstm(mq, mk, mv, og, gates, g_mlstm, L=L, rows=rows, init=mlstm_init)
    y_d = attend(dq, dk, dv)
    y = _ffn_block(x1, c_mod, g_norm, w2_in, w2_out, nb=nb, t=t, n_sub=n_sub, g_idx=(4, 5), m_idx=(6, 7, 8),
                   heads=(y_m, y_d), w_o=w_o, ym_transposed=v_transposed)
    return y, (k_state, v_state, c_new, n_new, m_new)


def kernel(x_prompt, x_sample, cache_k, cache_v, state_C, state_n, state_m, c_prompt, c_sample,
           w_ada, b_ada, g_norm, w_ffn1_in, w_ffn1_out, w_in, b_igate, b_fgate, g_mlstm, g_diff,
           lambda_qk, w_out, w_ffn2_in, w_ffn2_out):
    depth = w_ada.shape[0]
    bp, sp, d = x_prompt.shape
    bs, ts, _ = x_sample.shape
    past = cache_k.shape[2]
    hm, mv_dim = g_mlstm.shape[1:]
    hd, dv_dim = g_diff.shape[1:]
    mqk = mv_dim // 2
    dqk = dv_dim // 2
    sizes = [hm * mqk, hm * mqk, hm * mv_dim, hm * mv_dim, hm, hm, hd * 2 * dqk, hd * 2 * dqk, hd * dv_dim]
    offs = np.concatenate([[0], np.cumsum(sizes)])
    widths = tuple(sizes[i] for i in (0, 1, 2, 3, 6, 7, 8))
    t_prompt = min(512, sp)
    sub_prompt = 2 if sp % (2 * t_prompt) == 0 else 1
    l_prompt = min(256, sp)
    rows_prompt = max(r for r in (4, 2, 1) if bp % r == 0)

    y_p, y_s = x_prompt, x_sample
    states_p, states_s = [], []
    for l in range(depth):
        mod = _ada(jnp.concatenate([c_prompt, c_sample], axis=0), w_ada[l], b_ada[l])
        mod = mod.reshape(bp + bs, N_ADA, d)
        cols = [w_in[l][:, offs[i]:offs[i + 1]] for i in range(9)]
        gate_cols = jnp.concatenate([cols[4], cols[5], jnp.zeros((d, LANES - 2 * hm), F32)], axis=1)
        w_proj = jnp.concatenate([cols[i] for i in (0, 1, 6, 7, 8)] + [gate_cols], axis=1).astype(BF16)
        w_vo = jnp.concatenate([cols[2], cols[3]], axis=1).astype(BF16)
        bias = jnp.concatenate([b_igate[l], b_fgate[l], jnp.zeros((LANES - 2 * hm,), F32)]).reshape(1, LANES)
        weights = (g_norm[l], _interleave_gate_up(w_ffn1_in, l), w_ffn1_out[l].astype(BF16), w_proj,
                   w_vo, bias, g_mlstm[l], g_diff[l], lambda_qk[l], w_out[l].astype(BF16),
                   _interleave_gate_up(w_ffn2_in, l), w_ffn2_out[l].astype(BF16))
        dims = (hm, mqk, widths)

        attend_p = functools.partial(_attn_prompt, lambda_qk=lambda_qk[l], g_diff=g_diff[l],
                                     tq=min(256, sp), heads=2 if hd % 2 == 0 else 1, layer=l)
        y_p, st_p = _layer(y_p, mod[:bp], jnp.arange(sp), attend_p, None, weights, dims,
                           nb=1, t=t_prompt, n_sub=sub_prompt, L=l_prompt, rows=rows_prompt, layer=l)

        ck = cache_k[l].reshape(bs, past * hd, 2 * dqk)
        cv = cache_v[l].reshape(bs, past * hd, dv_dim)
        attend_s = functools.partial(_attn_sample, cache_k=ck, cache_v=cv, lambda_qk=lambda_qk[l],
                                     g_diff=g_diff[l], layer=l)
        init_s = (state_C[l], state_n[l], state_m[l].reshape(bs, hm, 1))
        y_s, st_s = _layer(y_s, mod[bp:], past + jnp.arange(ts), attend_s, init_s, weights, dims,
                           nb=bs, t=ts, n_sub=1, L=ts, rows=1, layer=l)
        states_p.append(st_p)
        states_s.append(st_s)

    def stack(states, b, s):
        if depth == 1:
            k, v, c, n, m = [a[None] for a in states[0]]
        else:
            k, v, c, n, m = [jnp.stack(a, axis=0) for a in zip(*states)]
        return (k.reshape(depth, b, s, hd, 2 * dqk), v.reshape(depth, b, s, hd, dv_dim),
                c, n, m.reshape(depth, b, hm))

    return (y_p, y_s) + stack(states_p, bp, sp) + stack(states_s, bs, ts)
```
